```python
import math
import jax, jax.numpy as jnp
from jax import lax
import numpy as np


D_MODEL = 1024
BATCH = 16
SEQ = 2048
DEPTH = 2

MEM_TOKENS = 256
HEAD_DIM = 64
MIX_WIDTH = D_MODEL
MLA_WIDTH = MIX_WIDTH // 2
MEM_WIDTH = MIX_WIDTH // 4
CONV_CH = MIX_WIDTH - MLA_WIDTH - MEM_WIDTH
N_MLA_HEADS = MLA_WIDTH // HEAD_DIM
N_MEM_HEADS = MEM_WIDTH // HEAD_DIM
QK_NOPE_DIM = 64
QK_ROPE_DIM = 32
QK_HEAD_DIM = QK_NOPE_DIM + QK_ROPE_DIM
V_HEAD_DIM = HEAD_DIM
Q_LORA_RANK = D_MODEL // 4
KV_LORA_RANK = D_MODEL // 8
CONV_K = 3
IN_SPLITS = (Q_LORA_RANK, KV_LORA_RANK, QK_ROPE_DIM, MEM_WIDTH, CONV_CH, CONV_CH, CONV_CH)
IN_PROJ_WIDTH = sum(IN_SPLITS)
D_FF = 2816
N_EXPERTS = 8
TOP_K = 2
EXPERT_FF = D_FF // TOP_K
N_DENSE = (DEPTH + 1) // 2
N_MOE = DEPTH // 2
Q_BLOCK = 128
ROPE_THETA = 10000.0
EPS = 1e-6

kernel_name = 'hybrid_mla_mem_shortconv_moe'


def rms_norm(x, g):
    xf = x.astype(jnp.float32)
    y = xf * lax.rsqrt(jnp.mean(xf * xf, axis=-1, keepdims=True) + EPS)
    return (y * g.astype(jnp.float32)).astype(x.dtype)


def rope_tables(seq, dtype):
    inv = 1.0 / (ROPE_THETA ** (jnp.arange(0, QK_ROPE_DIM, 2, dtype=jnp.float32) / QK_ROPE_DIM))
    ang = jnp.arange(seq, dtype=jnp.float32)[:, None] * inv[None, :]
    return jnp.cos(ang)[:, None, :].astype(dtype), jnp.sin(ang)[:, None, :].astype(dtype)


def apply_rope(x, cos, sin):
    x1, x2 = jnp.split(x, 2, axis=-1)
    return jnp.concatenate([x1 * cos - x2 * sin, x2 * cos + x1 * sin], axis=-1)


def causal_block_attention(q, k, v):
    B, S, H, Dq = q.shape
    nb = S // Q_BLOCK
    scale = 1.0 / math.sqrt(Dq)
    qb = q.reshape(B, nb, Q_BLOCK, H, Dq).transpose(1, 0, 2, 3, 4)
    kpos = jnp.arange(S)
    neg = jnp.finfo(jnp.float32).min

    def block(args):
        qi, start = args
        s = jnp.einsum('bqhd,bkhd->bhqk', qi, k).astype(jnp.float32) * scale
        qpos = start + jnp.arange(Q_BLOCK)
        s = jnp.where(kpos[None, :] <= qpos[:, None], s, neg)
        p = jax.nn.softmax(s, axis=-1).astype(v.dtype)
        return jnp.einsum('bhqk,bkhd->bqhd', p, v)

    out = lax.map(block, (qb, jnp.arange(nb) * Q_BLOCK))
    return out.transpose(1, 0, 2, 3, 4).reshape(B, S, H, v.shape[-1])


def causal_short_conv(u, w):
    C = u.shape[-1]
    return lax.conv_general_dilated(u, w[:, None, :], window_strides=(1,), padding=[(CONV_K - 1, 0)],
                                    dimension_numbers=('NWC', 'WIO', 'NWC'), feature_group_count=C)


def swiglu(h, w_gu, w_down):
    g, u = jnp.split(h @ w_gu, 2, axis=-1)
    return (jax.nn.silu(g) * u) @ w_down


def moe_swiglu(h, w_router, w_gu, w_down):
    logits = (h @ w_router).astype(jnp.float32)
    top_val, top_idx = lax.top_k(logits, TOP_K)
    top_w = jax.nn.softmax(top_val, axis=-1)
    combine = jnp.einsum('bsk,bske->bse', top_w,
                         jax.nn.one_hot(top_idx, N_EXPERTS, dtype=jnp.float32)).astype(h.dtype)
    y = jnp.zeros_like(h)
    for e in range(N_EXPERTS):
        y = y + combine[..., e:e + 1] * swiglu(h, w_gu[e], w_down[e])
    return y


def hybrid_mixer(h, mem_n, cos, sin, w_in, g_q_lat, w_q_up, g_kv_lat, w_kv_up, g_q_mla, g_k_mla,
                 w_mem_kv, g_q_mem, g_k_mem, conv_w, g_out, w_out):
    B, S, _ = h.shape
    H = N_MLA_HEADS
    z = h @ w_in
    offs = np.cumsum(IN_SPLITS)[:-1].tolist()
    q_lat, kv_lat, k_pe, q_mem, gate_b, gate_c, u = jnp.split(z, offs, axis=-1)

    q = (rms_norm(q_lat, g_q_lat) @ w_q_up).reshape(B, S, H, QK_HEAD_DIM)
    kv = (rms_norm(kv_lat, g_kv_lat) @ w_kv_up).reshape(B, S, H, QK_NOPE_DIM + V_HEAD_DIM)
    k_nope, v = kv[..., :QK_NOPE_DIM], kv[..., QK_NOPE_DIM:]
    q_nope = rms_norm(q[..., :QK_NOPE_DIM], g_q_mla[:QK_NOPE_DIM])
    q_pe = apply_rope(rms_norm(q[..., QK_NOPE_DIM:], g_q_mla[QK_NOPE_DIM:]), cos, sin)
    k_nope = rms_norm(k_nope, g_k_mla[:QK_NOPE_DIM])
    k_pe = apply_rope(rms_norm(k_pe, g_k_mla[QK_NOPE_DIM:])[:, :, None, :], cos, sin)
    qf = jnp.concatenate([q_nope, q_pe], axis=-1)
    kf = jnp.concatenate([k_nope, jnp.broadcast_to(k_pe, (B, S, H, QK_ROPE_DIM))], axis=-1)
    o_mla = causal_block_attention(qf, kf, v).reshape(B, S, MLA_WIDTH)

    km, vm = jnp.split(mem_n @ w_mem_kv, 2, axis=-1)
    M = mem_n.shape[1]
    km = rms_norm(km.reshape(B, M, N_MEM_HEADS, HEAD_DIM), g_k_mem)
    vm = vm.reshape(B, M, N_MEM_HEADS, HEAD_DIM)
    qm = rms_norm(q_mem.reshape(B, S, N_MEM_HEADS, HEAD_DIM), g_q_mem)
    s = jnp.einsum('bshd,bmhd->bhsm', qm, km).astype(jnp.float32) * (1.0 / math.sqrt(HEAD_DIM))
    p = jax.nn.softmax(s, axis=-1).astype(vm.dtype)
    o_mem = jnp.einsum('bhsm,bmhd->bshd', p, vm).reshape(B, S, MEM_WIDTH)

    o_conv = gate_b * causal_short_conv(gate_c * u, conv_w)

    a, b = MLA_WIDTH, MLA_WIDTH + MEM_WIDTH
    o = jnp.concatenate([rms_norm(o_mla, g_out[:a]), rms_norm(o_mem, g_out[a:b]),
                         rms_norm(o_conv, g_out[b:])], axis=-1)
    return o @ w_out


def setup_inputs(seed: int = 0) -> dict:
    key = jax.random.key(seed)
    ks = jax.random.split(key, 24)
    f32 = jnp.float32

    def nrm(k, shape, fan_in):
        return jax.random.normal(k, shape, f32) * (fan_in ** -0.5)

    def gain(k, shape):
        return 1.0 + 0.02 * jax.random.normal(k, shape, f32)

    L = DEPTH
    return {
        'x': jax.random.normal(ks[0], (BATCH, SEQ, D_MODEL), f32),
        'mem': jax.random.normal(ks[1], (BATCH, MEM_TOKENS, D_MODEL), f32),
        'g_mix': gain(ks[2], (L, D_MODEL)),
        'w_in': nrm(ks[3], (L, D_MODEL, IN_PROJ_WIDTH), D_MODEL),
        'g_q_lat': gain(ks[4], (L, Q_LORA_RANK)),
        'w_q_up': nrm(ks[5], (L, Q_LORA_RANK, N_MLA_HEADS * QK_HEAD_DIM), Q_LORA_RANK),
        'g_kv_lat': gain(ks[6], (L, KV_LORA_RANK)),
        'w_kv_up': nrm(ks[7], (L, KV_LORA_RANK, N_MLA_HEADS * (QK_NOPE_DIM + V_HEAD_DIM)), KV_LORA_RANK),
        'g_q_mla': gain(ks[8], (L, QK_HEAD_DIM)),
        'g_k_mla': gain(ks[9], (L, QK_HEAD_DIM)),
        'g_mem': gain(ks[10], (L, D_MODEL)),
        'w_mem_kv': nrm(ks[11], (L, D_MODEL, 2 * MEM_WIDTH), D_MODEL),
        'g_q_mem': gain(ks[12], (L, HEAD_DIM)),
        'g_k_mem': gain(ks[13], (L, HEAD_DIM)),
        'conv_w': nrm(ks[14], (L, CONV_K, CONV_CH), CONV_K),
        'g_out': gain(ks[15], (L, MIX_WIDTH)),
        'w_out': nrm(ks[16], (L, MIX_WIDTH, D_MODEL), MIX_WIDTH),
        'g_ffn': gain(ks[17], (L, D_MODEL)),
        'w_dense_gu': nrm(ks[18], (N_DENSE, D_MODEL, 2 * D_FF), D_MODEL),
        'w_dense_down': nrm(ks[19], (N_DENSE, D_FF, D_MODEL), D_FF),
        'w_router': nrm(ks[20], (N_MOE, D_MODEL, N_EXPERTS), D_MODEL),
        'w_expert_gu': nrm(ks[21], (N_MOE, N_EXPERTS, D_MODEL, 2 * EXPERT_FF), D_MODEL),
        'w_expert_down': nrm(ks[22], (N_MOE, N_EXPERTS, EXPERT_FF, D_MODEL), EXPERT_FF),
    }


def reference(x, mem, g_mix, w_in, g_q_lat, w_q_up, g_kv_lat, w_kv_up, g_q_mla, g_k_mla, g_mem,
              w_mem_kv, g_q_mem, g_k_mem, conv_w, g_out, w_out, g_ffn, w_dense_gu, w_dense_down,
              w_router, w_expert_gu, w_expert_down):
    cos, sin = rope_tables(x.shape[1], x.dtype)
    for l in range(DEPTH):
        h = rms_norm(x, g_mix[l])
        mem_n = rms_norm(mem, g_mem[l])
        x = x + hybrid_mixer(h, mem_n, cos, sin, w_in[l], g_q_lat[l], w_q_up[l], g_kv_lat[l], w_kv_up[l],
                             g_q_mla[l], g_k_mla[l], w_mem_kv[l], g_q_mem[l], g_k_mem[l], conv_w[l],
                             g_out[l], w_out[l])
        h = rms_norm(x, g_ffn[l])
        if l % 2 == 0:
            x = x + swiglu(h, w_dense_gu[l // 2], w_dense_down[l // 2])
        else:
            x = x + moe_swiglu(h, w_router[l // 2], w_expert_gu[l // 2], w_expert_down[l // 2])
    return x
```

```python
import functools
import math

import jax
import jax.numpy as jnp
from jax import lax
from jax.experimental import pallas as pl
from jax.experimental.pallas import tpu as pltpu

F32 = jnp.float32
BF16 = jnp.bfloat16

EPS = 1e-6
ROPE_THETA = 10000.0
HEAD_DIM = 64
ROPE_DIM = 32
N_MLA_HEADS = 8
N_MEM_HEADS = 4
CONV_K = 3
N_EXPERTS = 8
LANES = 128
HEAD_PAD = LANES
PAIR = 2 * LANES
CARRY_ROWS = 8
VMEM_LIMIT = 56 * 1024 * 1024

NT_DIMS = (((1,), (1,)), ((), ()))


def _rms(x, g):
    return x * lax.rsqrt(jnp.mean(x * x, axis=-1, keepdims=True) + EPS) * g


def _seg_sumsq(x, seg):
    w = seg.shape[0]
    x2 = (x * x).astype(BF16)
    parts = [jnp.dot(x2[:, c:c + w], seg, preferred_element_type=F32) for c in range(0, x.shape[1], w)]
    return parts[0] if len(parts) == 1 else jnp.concatenate(parts, axis=-1)


def _rope_head(y, cos, sin_signed, lane):
    up = pltpu.roll(y, LANES - ROPE_DIM // 2, 1)
    down = pltpu.roll(y, ROPE_DIM // 2, 1)
    swapped = jnp.where(lane < HEAD_DIM + ROPE_DIM // 2, up, down)
    return y * cos + swapped * sin_signed


def _mem_kv_kernel(mem_ref, g_mem_ref, w_ref, seg_ref, gk_ref, km_ref, vm_ref):
    width = km_ref.shape[-1]
    mn = _rms(mem_ref[...], g_mem_ref[...]).astype(BF16)
    kv = jnp.dot(mn, w_ref[...], preferred_element_type=F32)
    km = kv[:, :width]
    ss = _seg_sumsq(km, seg_ref[...])
    km_ref[...] = (km * lax.rsqrt(ss * (1.0 / HEAD_DIM) + EPS) * gk_ref[...]).astype(BF16)
    vm_ref[...] = kv[:, width:].astype(BF16)


def _mem_kv(mem, g_mem, w_mem_kv, seg64, gk_row):
    b, m, d = mem.shape
    width = w_mem_kv.shape[1] // 2
    const = lambda i: (0, 0)
    return pl.pallas_call(
        _mem_kv_kernel,
        grid=(b,),
        in_specs=[
            pl.BlockSpec((None, m, d), lambda i: (i, 0, 0)),
            pl.BlockSpec((1, d), const),
            pl.BlockSpec((d, 2 * width), const),
            pl.BlockSpec((PAIR, PAIR), const),
            pl.BlockSpec((1, width), const),
        ],
        out_specs=[
            pl.BlockSpec((None, m, width), lambda i: (i, 0, 0)),
            pl.BlockSpec((None, m, width), lambda i: (i, 0, 0)),
        ],
        out_shape=[jax.ShapeDtypeStruct((b, m, width), BF16)] * 2,
        compiler_params=pltpu.CompilerParams(dimension_semantics=("parallel",), vmem_limit_bytes=VMEM_LIMIT),
        name="mem_kv",
    )(mem, g_mem, w_mem_kv, seg64, gk_row)


def _proj_kernel(x_ref, g_mix_ref, w_in_ref, g_qlat_ref, w_q_ref, g_kvlat_ref, w_k_ref, w_v_ref,
                 seg_mla_ref, inv_mla_ref, gq_ref, gk_ref, gkpe_ref, cos_ref, sin_ref,
                 km_ref, vm_ref, seg64_ref, gqm_ref, conv_w_ref, g_omem_ref, g_oconv_ref,
                 q_ref, k_ref, v_ref, omem_ref, oconv_ref, cu_ref, *, tiles_per_seq, q_scale):
    tm = x_ref.shape[0]
    q_rank = g_qlat_ref.shape[1]
    kv_rank = g_kvlat_ref.shape[1]
    mem_w = gqm_ref.shape[1]
    conv_c = conv_w_ref.shape[1]

    h = _rms(x_ref[...], g_mix_ref[...]).astype(BF16)
    z = jnp.dot(h, w_in_ref[...], preferred_element_type=F32)
    o = 0
    q_lat = z[:, o:o + q_rank]; o += q_rank
    kv_lat = z[:, o:o + kv_rank]; o += kv_rank
    kpe = z[:, o:o + HEAD_PAD]; o += HEAD_PAD
    q_mem = z[:, o:o + mem_w]; o += mem_w
    gate_b = z[:, o:o + conv_c]; o += conv_c
    gate_c = z[:, o:o + conv_c]; o += conv_c
    u = z[:, o:o + conv_c]

    lane = lax.broadcasted_iota(jnp.int32, (tm, LANES), 1)
    cos = cos_ref[...]
    sin = sin_ref[...]
    seg_mla = seg_mla_ref[...]
    inv_mla = inv_mla_ref[...]

    qn = _rms(q_lat, g_qlat_ref[...]).astype(BF16)
    qf = jnp.dot(qn, w_q_ref[...], preferred_element_type=F32)
    n_rep = qf.shape[1] // PAIR
    inv_row = jnp.concatenate([inv_mla] * n_rep, axis=-1)
    qy = qf * lax.rsqrt(_seg_sumsq(qf, seg_mla) * inv_row + EPS) * gq_ref[...]
    for hd in range(qf.shape[1] // HEAD_PAD):
        sl = slice(hd * HEAD_PAD, (hd + 1) * HEAD_PAD)
        q_ref[:, sl] = (_rope_head(qy[:, sl], cos, sin, lane) * q_scale).astype(BF16)

    kvn = _rms(kv_lat, g_kvlat_ref[...]).astype(BF16)
    kf = jnp.dot(kvn, w_k_ref[...], preferred_element_type=F32)
    v_ref[...] = jnp.dot(kvn, w_v_ref[...], preferred_element_type=F32).astype(BF16)
    ky = kf * lax.rsqrt(_seg_sumsq(kf, seg_mla) * inv_row + EPS) * gk_ref[...]
    kpe_y = kpe * lax.rsqrt(_seg_sumsq(kpe, seg_mla[:LANES, :LANES]) * inv_mla[:, :LANES] + EPS) * gkpe_ref[...]
    kpe_r = _rope_head(kpe_y, cos, sin, lane)
    for hd in range(kf.shape[1] // HEAD_PAD):
        sl = slice(hd * HEAD_PAD, (hd + 1) * HEAD_PAD)
        k_ref[:, sl] = (ky[:, sl] + kpe_r).astype(BF16)

    qm = q_mem * lax.rsqrt(_seg_sumsq(q_mem, seg64_ref[...]) * (1.0 / HEAD_DIM) + EPS) * gqm_ref[...]
    km = km_ref[...]
    vm = vm_ref[...]
    lane_m = lax.broadcasted_iota(jnp.int32, (km.shape[0], LANES), 1)
    groups = []
    for grp in range(mem_w // LANES):
        sl = slice(grp * LANES, (grp + 1) * LANES)
        qg, kg, vg = qm[:, sl], km[:, sl], vm[:, sl]
        og = None
        for half in range(LANES // HEAD_DIM):
            in_head = (lane >= half * HEAD_DIM) & (lane < (half + 1) * HEAD_DIM)
            in_head_m = (lane_m >= half * HEAD_DIM) & (lane_m < (half + 1) * HEAD_DIM)
            qh = jnp.where(in_head, qg, 0.0).astype(BF16)
            s = lax.dot_general(qh, kg, NT_DIMS, preferred_element_type=F32) * (1.0 / math.sqrt(HEAD_DIM))
            p = jnp.exp(s - jnp.max(s, axis=-1, keepdims=True))
            p = (p * (1.0 / jnp.sum(p, axis=-1, keepdims=True))).astype(BF16)
            vh = jnp.where(in_head_m, vg, jnp.zeros_like(vg))
            oh = jnp.dot(p, vh, preferred_element_type=F32)
            og = oh if og is None else og + oh
        groups.append(og)
    o_mem = jnp.concatenate(groups, axis=-1)
    omem_ref[...] = _rms(o_mem, g_omem_ref[...]).astype(BF16)

    @pl.when(pl.program_id(0) % tiles_per_seq == 0)
    def _():
        cu_ref[0:CARRY_ROWS, :] = jnp.zeros((CARRY_ROWS, conv_c), F32)

    cu = gate_c * u
    cu_ref[CARRY_ROWS:CARRY_ROWS + tm, :] = cu
    w = conv_w_ref[...]
    conv = w[CONV_K - 1:CONV_K] * cu
    for tap in range(CONV_K - 1):
        shift = CONV_K - 1 - tap
        conv = conv + w[tap:tap + 1] * cu_ref[CARRY_ROWS - shift:CARRY_ROWS - shift + tm, :]
    oconv_ref[...] = _rms(gate_b * conv, g_oconv_ref[...]).astype(BF16)
    cu_ref[0:CARRY_ROWS, :] = cu_ref[tm:tm + CARRY_ROWS, :]


def _proj(x2d, seq, tm, lw, km, vm, cos, sin, consts):
    t, d = x2d.shape
    tiles_per_seq = seq // tm
    n_q = lw["w_q"].shape[1]
    n_v = lw["w_v"].shape[1]
    mem_w = km.shape[-1]
    conv_c = lw["conv_w"].shape[1]
    const = lambda i: (0, 0)
    row = lambda i: (i, 0)
    full = lambda a: pl.BlockSpec(a.shape, const)
    operands = [
        (x2d, pl.BlockSpec((tm, d), row)),
        (lw["g_mix"], None), (lw["w_in"], None), (lw["g_q_lat"], None), (lw["w_q"], None),
        (lw["g_kv_lat"], None), (lw["w_k"], None), (lw["w_v"], None),
        (consts["seg_mla"], None), (consts["inv_mla"], None),
        (lw["gq_row"], None), (lw["gk_row"], None), (lw["gkpe_row"], None),
        (cos, pl.BlockSpec((tm, LANES), lambda i: (i % tiles_per_seq, 0))),
        (sin, pl.BlockSpec((tm, LANES), lambda i: (i % tiles_per_seq, 0))),
        (km, pl.BlockSpec((None,) + km.shape[1:], lambda i: (i // tiles_per_seq, 0, 0))),
        (vm, pl.BlockSpec((None,) + vm.shape[1:], lambda i: (i // tiles_per_seq, 0, 0))),
        (consts["seg64"], None), (lw["gqm_row"], None), (lw["conv_w"], None),
        (lw["g_out_mem"], None), (lw["g_out_conv"], None),
    ]
    args = [a for a, _ in operands]
    specs = [full(a) if s is None else s for a, s in operands]
    out_widths = (n_q, n_q, n_v, mem_w, conv_c)
    return pl.pallas_call(
        functools.partial(_proj_kernel, tiles_per_seq=tiles_per_seq,
                          q_scale=1.0 / math.sqrt(HEAD_DIM + ROPE_DIM)),
        grid=(t // tm,),
        in_specs=specs,
        out_specs=[pl.BlockSpec((tm, w), row) for w in out_widths],
        out_shape=[jax.ShapeDtypeStruct((t, w), BF16) for w in out_widths],
        scratch_shapes=[pltpu.VMEM((tm + CARRY_ROWS, conv_c), F32)],
        compiler_params=pltpu.CompilerParams(dimension_semantics=("arbitrary",), vmem_limit_bytes=VMEM_LIMIT),
        name="proj",
    )(*args)


def _mla_attn_kernel(q_ref, k_ref, v_ref, o_ref):
    tq = q_ref.shape[0]
    qi = pl.program_id(2)
    neg = jnp.finfo(F32).min
    row = lax.broadcasted_iota(jnp.int32, (tq, tq), 0)
    col = lax.broadcasted_iota(jnp.int32, (tq, tq), 1)
    lane = lax.broadcasted_iota(jnp.int32, (tq, LANES), 1)

    outs = []
    for hh in range(PAIR // HEAD_PAD):
        sl = slice(hh * HEAD_PAD, (hh + 1) * HEAD_PAD)
        q = q_ref[:, sl]

        def block(j, carry, masked):
            m, l, acc = carry
            start = pl.multiple_of(j * tq, tq)
            kb = k_ref[pl.ds(start, tq), sl]
            vb = v_ref[pl.ds(start, tq), :]
            s = lax.dot_general(q, kb, NT_DIMS, preferred_element_type=F32)
            if masked:
                s = jnp.where(col <= row, s, neg)
            m_new = jnp.maximum(m, jnp.max(s, axis=-1, keepdims=True))
            alpha = jnp.exp(m - m_new)
            p = jnp.exp(s - m_new)
            l = alpha * l + jnp.sum(p, axis=-1, keepdims=True)
            acc = alpha * acc + jnp.dot(p.astype(BF16), vb, preferred_element_type=F32)
            return m_new, l, acc

        init = (jnp.full((tq, 1), -jnp.inf, F32), jnp.zeros((tq, 1), F32), jnp.zeros((tq, LANES), F32))
        carry = lax.fori_loop(0, qi, functools.partial(block, masked=False), init)
        _, l, acc = block(qi, carry, True)
        outs.append(acc * (1.0 / l))
    o_ref[...] = jnp.where(lane < HEAD_DIM, outs[0], outs[1]).astype(BF16)


def _mla_attn(q, k, v, batch, seq, tq):
    t = q.shape[0]
    n_pairs = q.shape[1] // PAIR
    nq = seq // tq
    return pl.pallas_call(
        _mla_attn_kernel,
        grid=(batch, n_pairs, nq),
        in_specs=[
            pl.BlockSpec((tq, PAIR), lambda b, hp, i: (b * nq + i, hp)),
            pl.BlockSpec((seq, PAIR), lambda b, hp, i: (b, hp)),
            pl.BlockSpec((seq, LANES), lambda b, hp, i: (b, hp)),
        ],
        out_specs=pl.BlockSpec((tq, LANES), lambda b, hp, i: (b * nq + i, hp)),
        out_shape=jax.ShapeDtypeStruct((t, n_pairs * LANES), BF16),
        compiler_params=pltpu.CompilerParams(dimension_semantics=("parallel", "parallel", "parallel"),
                                             vmem_limit_bytes=VMEM_LIMIT),
        name="mla_attn",
    )(q, k, v)


def _out_proj_kernel(x_ref, omla_ref, omem_ref, oconv_ref, g_ref, w_ref, y_ref):
    a = _rms(omla_ref[...].astype(F32), g_ref[...]).astype(BF16)
    cat = jnp.concatenate([a, omem_ref[...], oconv_ref[...]], axis=-1)
    y_ref[...] = x_ref[...] + jnp.dot(cat, w_ref[...], preferred_element_type=F32)


def _out_proj(x2d, o_mla, o_mem, o_conv, g_out_mla, w_out, tm):
    t, d = x2d.shape
    row = lambda i: (i, 0)
    const = lambda i: (0, 0)
    return pl.pallas_call(
        _out_proj_kernel,
        grid=(t // tm,),
        in_specs=[
            pl.BlockSpec((tm, d), row),
            pl.BlockSpec((tm, o_mla.shape[1]), row),
            pl.BlockSpec((tm, o_mem.shape[1]), row),
            pl.BlockSpec((tm, o_conv.shape[1]), row),
            pl.BlockSpec(g_out_mla.shape, const),
            pl.BlockSpec(w_out.shape, const),
        ],
        out_specs=pl.BlockSpec((tm, d), row),
        out_shape=jax.ShapeDtypeStruct((t, d), F32),
        compiler_params=pltpu.CompilerParams(dimension_semantics=("parallel",), vmem_limit_bytes=VMEM_LIMIT),
        name="out_proj",
    )(x2d, o_mla, o_mem, o_conv, g_out_mla, w_out)


def _swiglu_part(h, wg_ref, wu_ref, wd_ref, chunk):
    ff = wg_ref.shape[-1]
    out = None
    for c in range(0, ff, chunk):
        g = jnp.dot(h, wg_ref[:, c:c + chunk], preferred_element_type=F32)
        u = jnp.dot(h, wu_ref[:, c:c + chunk], preferred_element_type=F32)
        act = (g * (1.0 / (1.0 + jnp.exp(-g))) * u).astype(BF16)
        part = jnp.dot(act, wd_ref[c:c + chunk, :], preferred_element_type=F32)
        out = part if out is None else out + part
    return out


def _dense_ffn_kernel(x_ref, g_ref, wg_ref, wu_ref, wd_ref, y_ref, *, chunk):
    x = x_ref[...]
    h = _rms(x, g_ref[...]).astype(BF16)
    y_ref[...] = x + _swiglu_part(h, wg_ref, wu_ref, wd_ref, chunk)


def _dense_ffn(x2d, g_ffn, wg, wu, wd, tm, chunk):
    t, d = x2d.shape
    row = lambda i: (i, 0)
    const = lambda i: (0, 0)
    resident = lambda a: pl.BlockSpec(a.shape, const, pipeline_mode=pl.Buffered(1))
    return pl.pallas_call(
        functools.partial(_dense_ffn_kernel, chunk=chunk),
        grid=(t // tm,),
        in_specs=[pl.BlockSpec((tm, d), row), pl.BlockSpec(g_ffn.shape, const),
                  resident(wg), resident(wu), resident(wd)],
        out_specs=pl.BlockSpec((tm, d), row),
        out_shape=jax.ShapeDtypeStruct((t, d), F32),
        compiler_params=pltpu.CompilerParams(dimension_semantics=("parallel",), vmem_limit_bytes=VMEM_LIMIT),
        name="dense_ffn",
    )(x2d, g_ffn, wg, wu, wd)


def _moe_ffn_kernel(x_ref, g_ref, wr_ref, wg_ref, wu_ref, wd_ref, y_ref, h_scr, comb_scr, *, chunk):
    e = pl.program_id(1)
    tm = x_ref.shape[0]
    lane = lax.broadcasted_iota(jnp.int32, (tm, LANES), 1)

    @pl.when(e == 0)
    def _():
        x = x_ref[...]
        hf = _rms(x, g_ref[...])
        h_scr[...] = hf.astype(BF16)
        logits = jnp.dot(hf, wr_ref[...], preferred_element_type=F32, precision=lax.Precision.HIGHEST)
        lane_f = lane.astype(F32)
        logits = jnp.where(lane < N_EXPERTS, logits, -jnp.inf)
        m1 = jnp.max(logits, axis=-1, keepdims=True)
        i1 = jnp.min(jnp.where(logits == m1, lane_f, float(LANES)), axis=-1, keepdims=True)
        rest = jnp.where(lane_f == i1, -jnp.inf, logits)
        m2 = jnp.max(rest, axis=-1, keepdims=True)
        i2 = jnp.min(jnp.where(rest == m2, lane_f, float(LANES)), axis=-1, keepdims=True)
        e2 = jnp.exp(m2 - m1)
        w1 = 1.0 / (1.0 + e2)
        comb_scr[...] = jnp.where(lane_f == i1, w1, 0.0) + jnp.where(lane_f == i2, e2 * w1, 0.0)
        y_ref[...] = x

    c_e = jnp.sum(jnp.where(lane == e, comb_scr[...], 0.0), axis=-1, keepdims=True)
    y_ref[...] += c_e * _swiglu_part(h_scr[...], wg_ref, wu_ref, wd_ref, chunk)


def _moe_ffn(x2d, g_ffn, w_router, wg, wu, wd, tm, chunk):
    t, d = x2d.shape
    n_e, _, ff = wg.shape
    row = lambda i, e: (i, 0)
    const = lambda i, e: (0, 0)
    return pl.pallas_call(
        functools.partial(_moe_ffn_kernel, chunk=chunk),
        grid=(t // tm, n_e),
        in_specs=[
            pl.BlockSpec((tm, d), row),
            pl.BlockSpec(g_ffn.shape, const),
            pl.BlockSpec(w_router.shape, const),
            pl.BlockSpec((None, d, ff), lambda i, e: (e, 0, 0)),
            pl.BlockSpec((None, d, ff), lambda i, e: (e, 0, 0)),
            pl.BlockSpec((None, ff, d), lambda i, e: (e, 0, 0)),
        ],
        out_specs=pl.BlockSpec((tm, d), row),
        out_shape=jax.ShapeDtypeStruct((t, d), F32),
        scratch_shapes=[pltpu.VMEM((tm, d), BF16), pltpu.VMEM((tm, LANES), F32)],
        compiler_params=pltpu.CompilerParams(dimension_semantics=("parallel", "arbitrary"),
                                             vmem_limit_bytes=VMEM_LIMIT),
        name="moe_ffn",
    )(x2d, g_ffn, w_router, wg, wu, wd)


def _segment_matrix(bounds):
    seg_id = jnp.zeros((bounds[-1],), jnp.int32)
    for b in bounds[1:-1]:
        seg_id = seg_id + (jnp.arange(bounds[-1]) >= b).astype(jnp.int32)
    return (seg_id[:, None] == seg_id[None, :]).astype(BF16)


def _constants(seq):
    head_bounds = [0, HEAD_DIM, HEAD_DIM + ROPE_DIM, HEAD_PAD]
    pair_bounds = head_bounds + [HEAD_PAD + b for b in head_bounds[1:]]
    inv_head = jnp.concatenate([jnp.full((HEAD_DIM,), 1.0 / HEAD_DIM, F32),
                                jnp.full((HEAD_PAD - HEAD_DIM,), 1.0 / ROPE_DIM, F32)])
    half = ROPE_DIM // 2
    inv = 1.0 / (ROPE_THETA ** (jnp.arange(0, ROPE_DIM, 2, dtype=F32) / ROPE_DIM))
    ang = jnp.arange(seq, dtype=F32)[:, None] * inv[None, :]
    ones = jnp.ones((seq, HEAD_DIM), F32)
    zeros = jnp.zeros((seq, HEAD_DIM), F32)
    pad1 = jnp.ones((seq, HEAD_PAD - HEAD_DIM - ROPE_DIM), F32)
    cos = jnp.concatenate([ones, jnp.cos(ang), jnp.cos(ang), pad1], axis=-1)
    sin = jnp.concatenate([zeros, -jnp.sin(ang), jnp.sin(ang), 0.0 * pad1], axis=-1)
    assert cos.shape == (seq, HEAD_PAD) and half * 2 == ROPE_DIM
    return {
        "seg_mla": _segment_matrix(pair_bounds),
        "inv_mla": jnp.concatenate([inv_head, inv_head])[None, :],
        "seg64": _segment_matrix(list(range(0, PAIR + 1, HEAD_DIM))),
        "cos": cos,
        "sin": sin,
    }


def _layer_weights(l, p):
    d = p["w_in"].shape[1]
    q_rank = p["g_q_lat"].shape[1]
    kv_rank = p["g_kv_lat"].shape[1]
    qk_dim = HEAD_DIM + ROPE_DIM
    pad = HEAD_PAD - qk_dim
    w_in = p["w_in"][l]
    o = q_rank + kv_rank
    w_kpe = w_in[:, o:o + ROPE_DIM]
    w_in_p = jnp.concatenate([
        w_in[:, :o],
        jnp.zeros((d, HEAD_DIM), F32), w_kpe, jnp.zeros((d, pad), F32),
        w_in[:, o + ROPE_DIM:],
    ], axis=1).astype(BF16)

    w_q = p["w_q_up"][l].reshape(q_rank, N_MLA_HEADS, qk_dim)
    w_q = jnp.pad(w_q, ((0, 0), (0, 0), (0, pad))).reshape(q_rank, N_MLA_HEADS * HEAD_PAD).astype(BF16)
    w_kv = p["w_kv_up"][l].reshape(kv_rank, N_MLA_HEADS, 2 * HEAD_DIM)
    w_k = jnp.pad(w_kv[:, :, :HEAD_DIM], ((0, 0), (0, 0), (0, HEAD_PAD - HEAD_DIM)))
    w_k = w_k.reshape(kv_rank, N_MLA_HEADS * HEAD_PAD).astype(BF16)
    w_v = w_kv[:, :, HEAD_DIM:].reshape(kv_rank, N_MLA_HEADS * HEAD_DIM).astype(BF16)

    gq = jnp.pad(p["g_q_mla"][l], (0, pad))
    gk = jnp.pad(p["g_k_mla"][l][:HEAD_DIM], (0, HEAD_PAD - HEAD_DIM))
    gkpe = jnp.pad(p["g_k_mla"][l][HEAD_DIM:], (HEAD_DIM, pad))
    g_out = p["g_out"][l]
    mla_w = N_MLA_HEADS * HEAD_DIM
    mem_w = N_MEM_HEADS * HEAD_DIM
    return {
        "g_mix": p["g_mix"][l][None, :],
        "w_in": w_in_p,
        "g_q_lat": p["g_q_lat"][l][None, :],
        "w_q": w_q,
        "g_kv_lat": p["g_kv_lat"][l][None, :],
        "w_k": w_k,
        "w_v": w_v,
        "gq_row": jnp.tile(gq, N_MLA_HEADS)[None, :],
        "gk_row": jnp.tile(gk, N_MLA_HEADS)[None, :],
        "gkpe_row": gkpe[None, :],
        "g_mem": p["g_mem"][l][None, :],
        "w_mem_kv": p["w_mem_kv"][l].astype(BF16),
        "gqm_row": jnp.tile(p["g_q_mem"][l], N_MEM_HEADS)[None, :],
        "gkm_row": jnp.tile(p["g_k_mem"][l], N_MEM_HEADS)[None, :],
        "conv_w": p["conv_w"][l],
        "g_out_mla": g_out[None, :mla_w],
        "g_out_mem": g_out[None, mla_w:mla_w + mem_w],
        "g_out_conv": g_out[None, mla_w + mem_w:],
        "w_out": p["w_out"][l].astype(BF16),
        "g_ffn": p["g_ffn"][l][None, :],
    }


def _tile_config(seq):
    return dict(tm=min(512, seq), tq=min(512, seq), tm_ffn=512, ffn_chunk=1408, moe_chunk=1408)


def _forward(p, *, tm, tq, tm_ffn, ffn_chunk, moe_chunk):
    x = p["x"]
    batch, seq, d = x.shape
    depth = p["g_mix"].shape[0]
    assert seq % tm == 0 and seq % tq == 0 and (batch * seq) % tm_ffn == 0
    consts = _constants(seq)
    x2d = x.reshape(batch * seq, d)
    for l in range(depth):
        lw = _layer_weights(l, p)
        km, vm = _mem_kv(p["mem"], lw["g_mem"], lw["w_mem_kv"], consts["seg64"], lw["gkm_row"])
        q, k, v, o_mem, o_conv = _proj(x2d, seq, tm, lw, km, vm, consts["cos"], consts["sin"], consts)
        o_mla = _mla_attn(q, k, v, batch, seq, tq)
        x2d = _out_proj(x2d, o_mla, o_mem, o_conv, lw["g_out_mla"], lw["w_out"], tm)
        if l % 2 == 0:
            w_gu = p["w_dense_gu"][l // 2]
            ff = w_gu.shape[1] // 2
            x2d = _dense_ffn(x2d, lw["g_ffn"], w_gu[:, :ff].astype(BF16), w_gu[:, ff:].astype(BF16),
                             p["w_dense_down"][l // 2].astype(BF16), tm_ffn, ffn_chunk)
        else:
            w_gu = p["w_expert_gu"][l // 2]
            ff = w_gu.shape[2] // 2
            w_router = jnp.pad(p["w_router"][l // 2], ((0, 0), (0, LANES - N_EXPERTS)))
            x2d = _moe_ffn(x2d, lw["g_ffn"], w_router, w_gu[:, :, :ff].astype(BF16), w_gu[:, :, ff:].astype(BF16),
                           p["w_expert_down"][l // 2].astype(BF16), tm_ffn, moe_chunk)
    return x2d.reshape(batch, seq, d)


def kernel(x, mem, g_mix, w_in, g_q_lat, w_q_up, g_kv_lat, w_kv_up, g_q_mla, g_k_mla, g_mem, w_mem_kv, g_q_mem,
           g_k_mem, conv_w, g_out, w_out, g_ffn, w_dense_gu, w_dense_down, w_router, w_expert_gu, w_expert_down):
    p = dict(x=x, mem=mem, g_mix=g_mix, w_in=w_in, g_q_lat=g_q_lat, w_q_up=w_q_up, g_kv_lat=g_kv_lat,
             w_kv_up=w_kv_up, g_q_mla=g_q_mla, g_k_mla=g_k_mla, g_mem=g_mem, w_mem_kv=w_mem_kv, g_q_mem=g_q_mem,
             g_k_mem=g_k_mem, conv_w=conv_w, g_out=g_out, w_out=w_out, g_ffn=g_ffn, w_dense_gu=w_dense_gu,
             w_dense_down=w_dense_down, w_router=w_router, w_expert_gu=w_expert_gu, w_expert_down=w_expert_down)
    return _forward(p, **_tile_config(x.shape[1]))
```

```python
import functools
import math

import jax
import jax.numpy as jnp
from jax import lax
from jax.experimental import pallas as pl
from jax.experimental.pallas import tpu as pltpu

F32 = jnp.float32
BF16 = jnp.bfloat16

EPS = 1e-6
ROPE_THETA = 10000.0
HEAD_DIM = 64
ROPE_DIM = 32
N_MLA_HEADS = 8
N_MEM_HEADS = 4
CONV_K = 3
N_EXPERTS = 8
LANES = 128
HEAD_PAD = LANES
PAIR = 2 * LANES
CARRY_ROWS = 8
VMEM_LIMIT = 56 * 1024 * 1024

NT_DIMS = (((1,), (1,)), ((), ()))


def _rms(x, g):
    return x * lax.rsqrt(jnp.mean(x * x, axis=-1, keepdims=True) + EPS) * g


def _seg_sumsq(x, seg):
    w = seg.shape[0]
    x2 = (x * x).astype(BF16)
    parts = [jnp.dot(x2[:, c:c + w], seg, preferred_element_type=F32) for c in range(0, x.shape[1], w)]
    return parts[0] if len(parts) == 1 else jnp.concatenate(parts, axis=-1)


def _rope_head(y, cos, sin_signed, lane):
    up = pltpu.roll(y, LANES - ROPE_DIM // 2, 1)
    down = pltpu.roll(y, ROPE_DIM // 2, 1)
    swapped = jnp.where(lane < HEAD_DIM + ROPE_DIM // 2, up, down)
    return y * cos + swapped * sin_signed


def _mem_kv_kernel(mem_ref, g_mem_ref, w_ref, seg_ref, gk_ref, km_ref, vm_ref):
    width = km_ref.shape[-1]
    mn = _rms(mem_ref[...], g_mem_ref[...]).astype(BF16)
    kv = jnp.dot(mn, w_ref[...], preferred_element_type=F32)
    km = kv[:, :width]
    ss = _seg_sumsq(km, seg_ref[...])
    km_ref[...] = (km * lax.rsqrt(ss * (1.0 / HEAD_DIM) + EPS) * gk_ref[...]).astype(BF16)
    vm_ref[...] = kv[:, width:].astype(BF16)


def _mem_kv(mem, g_mem, w_mem_kv, seg64, gk_row):
    b, m, d = mem.shape
    width = w_mem_kv.shape[1] // 2
    const = lambda i: (0, 0)
    return pl.pallas_call(
        _mem_kv_kernel,
        grid=(b,),
        in_specs=[
            pl.BlockSpec((None, m, d), lambda i: (i, 0, 0)),
            pl.BlockSpec((1, d), const),
            pl.BlockSpec((d, 2 * width), const),
            pl.BlockSpec((PAIR, PAIR), const),
            pl.BlockSpec((1, width), const),
        ],
        out_specs=[
            pl.BlockSpec((None, m, width), lambda i: (i, 0, 0)),
            pl.BlockSpec((None, m, width), lambda i: (i, 0, 0)),
        ],
        out_shape=[jax.ShapeDtypeStruct((b, m, width), BF16)] * 2,
        compiler_params=pltpu.CompilerParams(dimension_semantics=("parallel",), vmem_limit_bytes=VMEM_LIMIT),
        name="mem_kv",
    )(mem, g_mem, w_mem_kv, seg64, gk_row)


def _proj_kernel(x_ref, g_mix_ref, w_in_ref, g_qlat_ref, w_q_ref, g_kvlat_ref, w_k_ref, w_v_ref,
                 seg_mla_ref, inv_mla_ref, gq_ref, gk_ref, gkpe_ref, cos_ref, sin_ref,
                 km_ref, vm_ref, seg64_ref, gqm_ref, conv_w_ref, g_omem_ref, g_oconv_ref,
                 q_ref, k_ref, v_ref, omem_ref, oconv_ref, cu_ref, *, tiles_per_seq, q_scale):
    tm = x_ref.shape[0]
    q_rank = g_qlat_ref.shape[1]
    kv_rank = g_kvlat_ref.shape[1]
    mem_w = gqm_ref.shape[1]
    conv_c = conv_w_ref.shape[1]

    h = _rms(x_ref[...], g_mix_ref[...]).astype(BF16)
    z = jnp.dot(h, w_in_ref[...], preferred_element_type=F32)
    o = 0
    q_lat = z[:, o:o + q_rank]; o += q_rank
    kv_lat = z[:, o:o + kv_rank]; o += kv_rank
    kpe = z[:, o:o + HEAD_PAD]; o += HEAD_PAD
    q_mem = z[:, o:o + mem_w]; o += mem_w
    gate_b = z[:, o:o + conv_c]; o += conv_c
    gate_c = z[:, o:o + conv_c]; o += conv_c
    u = z[:, o:o + conv_c]

    lane = lax.broadcasted_iota(jnp.int32, (tm, LANES), 1)
    cos = cos_ref[...]
    sin = sin_ref[...]
    seg_mla = seg_mla_ref[...]
    inv_mla = inv_mla_ref[...]

    qn = _rms(q_lat, g_qlat_ref[...]).astype(BF16)
    qf = jnp.dot(qn, w_q_ref[...], preferred_element_type=F32)
    n_rep = qf.shape[1] // PAIR
    inv_row = jnp.concatenate([inv_mla] * n_rep, axis=-1)
    qy = qf * lax.rsqrt(_seg_sumsq(qf, seg_mla) * inv_row + EPS) * gq_ref[...]
    for hd in range(qf.shape[1] // HEAD_PAD):
        sl = slice(hd * HEAD_PAD, (hd + 1) * HEAD_PAD)
        q_ref[:, sl] = (_rope_head(qy[:, sl], cos, sin, lane) * q_scale).astype(BF16)

    kvn = _rms(kv_lat, g_kvlat_ref[...]).astype(BF16)
    kf = jnp.dot(kvn, w_k_ref[...], preferred_element_type=F32)
    v_ref[...] = jnp.dot(kvn, w_v_ref[...], preferred_element_type=F32).astype(BF16)
    ky = kf * lax.rsqrt(_seg_sumsq(kf, seg_mla) * inv_row + EPS) * gk_ref[...]
    kpe_y = kpe * lax.rsqrt(_seg_sumsq(kpe, seg_mla[:LANES, :LANES]) * inv_mla[:, :LANES] + EPS) * gkpe_ref[...]
    kpe_r = _rope_head(kpe_y, cos, sin, lane)
    for hd in range(kf.shape[1] // HEAD_PAD):
        sl = slice(hd * HEAD_PAD, (hd + 1) * HEAD_PAD)
        k_ref[:, sl] = (ky[:, sl] + kpe_r).astype(BF16)

    qm = q_mem * lax.rsqrt(_seg_sumsq(q_mem, seg64_ref[...]) * (1.0 / HEAD_DIM) + EPS) * gqm_ref[...]
    km = km_ref[...]
    vm = vm_ref[...]
    lane_m = lax.broadcasted_iota(jnp.int32, (km.shape[0], LANES), 1)
    groups = []
    for grp in range(mem_w // LANES):
        sl = slice(grp * LANES, (grp + 1) * LANES)
        qg, kg, vg = qm[:, sl], km[:, sl], vm[:, sl]
        og = None
        for half in range(LANES // HEAD_DIM):
            in_head = (lane >= half * HEAD_DIM) & (lane < (half + 1) * HEAD_DIM)
            in_head_m = (lane_m >= half * HEAD_DIM) & (lane_m < (half + 1) * HEAD_DIM)
            qh = jnp.where(in_head, qg, 0.0).astype(BF16)
            s = lax.dot_general(qh, kg, NT_DIMS, preferred_element_type=F32) * (1.0 / math.sqrt(HEAD_DIM))
            p = jnp.exp(s - jnp.max(s, axis=-1, keepdims=True))
            p = (p * (1.0 / jnp.sum(p, axis=-1, keepdims=True))).astype(BF16)
            vh = jnp.where(in_head_m, vg, jnp.zeros_like(vg))
            oh = jnp.dot(p, vh, preferred_element_type=F32)
            og = oh if og is None else og + oh
        groups.append(og)
    o_mem = jnp.concatenate(groups, axis=-1)
    omem_ref[...] = _rms(o_mem, g_omem_ref[...]).astype(BF16)

    @pl.when(pl.program_id(0) % tiles_per_seq == 0)
    def _():
        cu_ref[0:CARRY_ROWS, :] = jnp.zeros((CARRY_ROWS, conv_c), F32)

    cu = gate_c * u
    cu_ref[CARRY_ROWS:CARRY_ROWS + tm, :] = cu
    w = conv_w_ref[...]
    conv = w[CONV_K - 1:CONV_K] * cu
    for tap in range(CONV_K - 1):
        shift = CONV_K - 1 - tap
        conv = conv + w[tap:tap + 1] * cu_ref[CARRY_ROWS - shift:CARRY_ROWS - shift + tm, :]
    oconv_ref[...] = _rms(gate_b * conv, g_oconv_ref[...]).astype(BF16)
    cu_ref[0:CARRY_ROWS, :] = cu_ref[tm:tm + CARRY_ROWS, :]


def _proj(x2d, seq, tm, lw, km, vm, cos, sin, consts):
    t, d = x2d.shape
    tiles_per_seq = seq // tm
    n_q = lw["w_q"].shape[1]
    n_v = lw["w_v"].shape[1]
    mem_w = km.shape[-1]
    conv_c = lw["conv_w"].shape[1]
    const = lambda i: (0, 0)
    row = lambda i: (i, 0)
    full = lambda a: pl.BlockSpec(a.shape, const)
    operands = [
        (x2d, pl.BlockSpec((tm, d), row)),
        (lw["g_mix"], None), (lw["w_in"], None), (lw["g_q_lat"], None), (lw["w_q"], None),
        (lw["g_kv_lat"], None), (lw["w_k"], None), (lw["w_v"], None),
        (consts["seg_mla"], None), (consts["inv_mla"], None),
        (lw["gq_row"], None), (lw["gk_row"], None), (lw["gkpe_row"], None),
        (cos, pl.BlockSpec((tm, LANES), lambda i: (i % tiles_per_seq, 0))),
        (sin, pl.BlockSpec((tm, LANES), lambda i: (i % tiles_per_seq, 0))),
        (km, pl.BlockSpec((None,) + km.shape[1:], lambda i: (i // tiles_per_seq, 0, 0))),
        (vm, pl.BlockSpec((None,) + vm.shape[1:], lambda i: (i // tiles_per_seq, 0, 0))),
        (consts["seg64"], None), (lw["gqm_row"], None), (lw["conv_w"], None),
        (lw["g_out_mem"], None), (lw["g_out_conv"], None),
    ]
    args = [a for a, _ in operands]
    specs = [full(a) if s is None else s for a, s in operands]
    out_widths = (n_q, n_q, n_v, mem_w, conv_c)
    return pl.pallas_call(
        functools.partial(_proj_kernel, tiles_per_seq=tiles_per_seq,
                          q_scale=1.0 / math.sqrt(HEAD_DIM + ROPE_DIM)),
        grid=(t // tm,),
        in_specs=specs,
        out_specs=[pl.BlockSpec((tm, w), row) for w in out_widths],
        out_shape=[jax.ShapeDtypeStruct((t, w), BF16) for w in out_widths],
        scratch_shapes=[pltpu.VMEM((tm + CARRY_ROWS, conv_c), F32)],
        compiler_params=pltpu.CompilerParams(dimension_semantics=("arbitrary",), vmem_limit_bytes=VMEM_LIMIT),
        name="proj",
    )(*args)


def _mla_attn_kernel(q_ref, k_ref, v_ref, o_ref):
    tq = q_ref.shape[0]
    qi = pl.program_id(2)
    neg = jnp.finfo(F32).min
    row = lax.broadcasted_iota(jnp.int32, (tq, tq), 0)
    col = lax.broadcasted_iota(jnp.int32, (tq, tq), 1)
    lane = lax.broadcasted_iota(jnp.int32, (tq, LANES), 1)

    n_heads = PAIR // HEAD_PAD
    head_lanes = [slice(hh * HEAD_PAD, (hh + 1) * HEAD_PAD) for hh in range(n_heads)]
    qs = [q_ref[:, sl] for sl in head_lanes]

    def block(j, carry, masked):
        start = pl.multiple_of(j * tq, tq)
        vb = v_ref[pl.ds(start, tq), :]
        new = []
        for hh in range(n_heads):
            m, l, acc = carry[hh]
            kb = k_ref[pl.ds(start, tq), head_lanes[hh]]
            s = lax.dot_general(qs[hh], kb, NT_DIMS, preferred_element_type=F32)
            if masked:
                s = jnp.where(col <= row, s, neg)
            m_new = jnp.maximum(m, jnp.max(s, axis=-1, keepdims=True))
            alpha = jnp.exp(m - m_new)
            p = jnp.exp(s - m_new)
            l = alpha * l + jnp.sum(p, axis=-1, keepdims=True)
            acc = alpha * acc + jnp.dot(p.astype(BF16), vb, preferred_element_type=F32)
            new.append((m_new, l, acc))
        return tuple(new)

    init = tuple((jnp.full((tq, 1), -jnp.inf, F32), jnp.zeros((tq, 1), F32), jnp.zeros((tq, LANES), F32))
                 for _ in range(n_heads))
    carry = lax.fori_loop(0, qi, functools.partial(block, masked=False), init)
    carry = block(qi, carry, True)
    outs = [acc * (1.0 / l) for _, l, acc in carry]
    o_ref[...] = jnp.where(lane < HEAD_DIM, outs[0], outs[1]).astype(BF16)


def _mla_attn(q, k, v, batch, seq, tq):
    t = q.shape[0]
    n_pairs = q.shape[1] // PAIR
    nq = seq // tq
    return pl.pallas_call(
        _mla_attn_kernel,
        grid=(batch, n_pairs, nq),
        in_specs=[
            pl.BlockSpec((tq, PAIR), lambda b, hp, i: (b * nq + i, hp)),
            pl.BlockSpec((seq, PAIR), lambda b, hp, i: (b, hp)),
            pl.BlockSpec((seq, LANES), lambda b, hp, i: (b, hp)),
        ],
        out_specs=pl.BlockSpec((tq, LANES), lambda b, hp, i: (b * nq + i, hp)),
        out_shape=jax.ShapeDtypeStruct((t, n_pairs * LANES), BF16),
        compiler_params=pltpu.CompilerParams(dimension_semantics=("parallel", "parallel", "parallel"),
                                             vmem_limit_bytes=VMEM_LIMIT),
        name="mla_attn",
    )(q, k, v)


def _out_proj_kernel(x_ref, omla_ref, omem_ref, oconv_ref, g_ref, w_ref, y_ref):
    a = _rms(omla_ref[...].astype(F32), g_ref[...]).astype(BF16)
    cat = jnp.concatenate([a, omem_ref[...], oconv_ref[...]], axis=-1)
    y_ref[...] = x_ref[...] + jnp.dot(cat, w_ref[...], preferred_element_type=F32)


def _out_proj(x2d, o_mla, o_mem, o_conv, g_out_mla, w_out, tm):
    t, d = x2d.shape
    row = lambda i: (i, 0)
    const = lambda i: (0, 0)
    return pl.pallas_call(
        _out_proj_kernel,
        grid=(t // tm,),
        in_specs=[
            pl.BlockSpec((tm, d), row),
            pl.BlockSpec((tm, o_mla.shape[1]), row),
            pl.BlockSpec((tm, o_mem.shape[1]), row),
            pl.BlockSpec((tm, o_conv.shape[1]), row),
            pl.BlockSpec(g_out_mla.shape, const),
            pl.BlockSpec(w_out.shape, const),
        ],
        out_specs=pl.BlockSpec((tm, d), row),
        out_shape=jax.ShapeDtypeStruct((t, d), F32),
        compiler_params=pltpu.CompilerParams(dimension_semantics=("parallel",), vmem_limit_bytes=VMEM_LIMIT),
        name="out_proj",
    )(x2d, o_mla, o_mem, o_conv, g_out_mla, w_out)


def _swiglu_part(h, wg_ref, wu_ref, wd_ref, chunk):
    ff = wg_ref.shape[-1]
    out = None
    for c in range(0, ff, chunk):
        g = jnp.dot(h, wg_ref[:, c:c + chunk], preferred_element_type=F32)
        u = jnp.dot(h, wu_ref[:, c:c + chunk], preferred_element_type=F32)
        act = (g * (1.0 / (1.0 + jnp.exp(-g))) * u).astype(BF16)
        part = jnp.dot(act, wd_ref[c:c + chunk, :], preferred_element_type=F32)
        out = part if out is None else out + part
    return out


def _dense_ffn_kernel(x_ref, g_ref, wg_ref, wu_ref, wd_ref, y_ref, *, chunk):
    x = x_ref[...]
    h = _rms(x, g_ref[...]).astype(BF16)
    y_ref[...] = x + _swiglu_part(h, wg_ref, wu_ref, wd_ref, chunk)


def _dense_ffn(x2d, g_ffn, wg, wu, wd, tm, chunk):
    t, d = x2d.shape
    row = lambda i: (i, 0)
    const = lambda i: (0, 0)
    resident = lambda a: pl.BlockSpec(a.shape, const, pipeline_mode=pl.Buffered(1))
    return pl.pallas_call(
        functools.partial(_dense_ffn_kernel, chunk=chunk),
        grid=(t // tm,),
        in_specs=[pl.BlockSpec((tm, d), row), pl.BlockSpec(g_ffn.shape, const),
                  resident(wg), resident(wu), resident(wd)],
        out_specs=pl.BlockSpec((tm, d), row),
        out_shape=jax.ShapeDtypeStruct((t, d), F32),
        compiler_params=pltpu.CompilerParams(dimension_semantics=("parallel",), vmem_limit_bytes=VMEM_LIMIT),
        name="dense_ffn",
    )(x2d, g_ffn, wg, wu, wd)


ROUTE_ROWS = 8
TOP_K = 2
DMA_UNROLL = 8


def _moe_route_kernel(x_ref, g_ref, wr_ref, tri_ref, rw_ref, ri_ref, cnt_ref, base_scr):
    tm = x_ref.shape[0]
    lane = lax.broadcasted_iota(jnp.int32, (tm, LANES), 1)
    lane_f = lane.astype(F32)

    @pl.when(pl.program_id(0) == 0)
    def _():
        base_scr[...] = jnp.zeros_like(base_scr)

    hf = _rms(x_ref[...], g_ref[...])
    logits = jnp.dot(hf, wr_ref[...], preferred_element_type=F32, precision=lax.Precision.HIGHEST)
    logits = jnp.where(lane < N_EXPERTS, logits, -jnp.inf)
    m1 = jnp.max(logits, axis=-1, keepdims=True)
    i1 = jnp.min(jnp.where(logits == m1, lane_f, float(LANES)), axis=-1, keepdims=True)
    rest = jnp.where(lane_f == i1, -jnp.inf, logits)
    m2 = jnp.max(rest, axis=-1, keepdims=True)
    i2 = jnp.min(jnp.where(rest == m2, lane_f, float(LANES)), axis=-1, keepdims=True)
    e2 = jnp.exp(m2 - m1)
    w1 = 1.0 / (1.0 + e2)
    rw_ref[...] = jnp.where(lane == 0, w1, jnp.where(lane == 1, e2 * w1, 0.0))

    chosen = jnp.where((lane_f == i1) | (lane_f == i2), 1.0, 0.0)
    before = jnp.dot(tri_ref[...], chosen.astype(BF16), preferred_element_type=F32) + base_scr[...]
    r1 = jnp.sum(jnp.where(lane_f == i1, before, 0.0), axis=-1, keepdims=True)
    r2 = jnp.sum(jnp.where(lane_f == i2, before, 0.0), axis=-1, keepdims=True)
    base_scr[...] += jnp.sum(chosen, axis=0, keepdims=True)
    cnt_ref[...] = base_scr[...].astype(jnp.int32)
    table = jnp.where(lane == 0, i1, jnp.where(lane == 1, i2, jnp.where(lane == 2, r1, jnp.where(lane == 3, r2, 0.0))))
    ri_ref[...] = table.T[:ROUTE_ROWS, :].astype(jnp.int32)


def _moe_route(x2d, g_ffn, w_router, tm):
    t, d = x2d.shape
    tri = (jnp.arange(tm)[:, None] > jnp.arange(tm)[None, :]).astype(BF16)
    const = lambda i: (0, 0)
    return pl.pallas_call(
        _moe_route_kernel,
        grid=(t // tm,),
        in_specs=[pl.BlockSpec((tm, d), lambda i: (i, 0)), pl.BlockSpec(g_ffn.shape, const),
                  pl.BlockSpec(w_router.shape, const), pl.BlockSpec((tm, tm), const)],
        out_specs=[pl.BlockSpec((tm, LANES), lambda i: (i, 0)),
                   pl.BlockSpec((None, ROUTE_ROWS, tm), lambda i: (i, 0, 0)),
                   pl.BlockSpec((1, LANES), const)],
        out_shape=[jax.ShapeDtypeStruct((t, LANES), F32),
                   jax.ShapeDtypeStruct((t // tm, ROUTE_ROWS, tm), jnp.int32),
                   jax.ShapeDtypeStruct((1, LANES), jnp.int32)],
        scratch_shapes=[pltpu.VMEM((1, LANES), F32)],
        compiler_params=pltpu.CompilerParams(dimension_semantics=("arbitrary",), vmem_limit_bytes=VMEM_LIMIT),
        name="moe_route",
    )(x2d, g_ffn, w_router, tri)


def _row_copy(src_ref, src_row, dst_ref, dst_row, sem):
    return pltpu.make_async_copy(src_ref.at[pl.ds(src_row, 1)], dst_ref.at[pl.ds(dst_row, 1)], sem)


def _moe_dispatch_kernel(off_ref, cnt_ref, end_ref, x_ref, g_ref, ri_ref, xs_ref, h_scr, zero_scr, sem):
    tm = x_ref.shape[0]
    h_scr[...] = _rms(x_ref[...], g_ref[...])

    def start(r, c):
        for k in range(TOP_K):
            dst = off_ref[ri_ref[k, r]] + ri_ref[TOP_K + k, r]
            _row_copy(h_scr, r, xs_ref, dst, sem).start()
        return c

    def wait(r, c):
        for k in range(TOP_K):
            _row_copy(h_scr, 0, xs_ref, 0, sem).wait()
        return c

    lax.fori_loop(0, tm, start, 0, unroll=DMA_UNROLL)
    lax.fori_loop(0, tm, wait, 0, unroll=DMA_UNROLL)

    @pl.when(pl.program_id(0) == pl.num_programs(0) - 1)
    def _():
        zero_scr[...] = jnp.zeros_like(zero_scr)
        for e in range(N_EXPERTS):
            lo = off_ref[e] + cnt_ref[e]
            hi = end_ref[e]

            def zstart(r, c):
                _row_copy(zero_scr, 0, xs_ref, r, sem).start()
                return c

            def zwait(r, c):
                _row_copy(zero_scr, 0, xs_ref, 0, sem).wait()
                return c

            lax.fori_loop(lo, hi, zstart, 0)
            lax.fori_loop(lo, hi, zwait, 0)


def _moe_dispatch(x2d, g_ffn, route_i, off, cnt, end, n_rows, tm):
    t, d = x2d.shape
    return pl.pallas_call(
        _moe_dispatch_kernel,
        grid_spec=pltpu.PrefetchScalarGridSpec(
            num_scalar_prefetch=3,
            grid=(t // tm,),
            in_specs=[pl.BlockSpec((tm, d), lambda i, *_: (i, 0)),
                      pl.BlockSpec(g_ffn.shape, lambda i, *_: (0, 0)),
                      pl.BlockSpec((None, ROUTE_ROWS, tm), lambda i, *_: (i, 0, 0), memory_space=pltpu.SMEM)],
            out_specs=pl.BlockSpec(memory_space=pl.ANY),
            scratch_shapes=[pltpu.VMEM((tm, d), F32), pltpu.VMEM((CARRY_ROWS, d), F32), pltpu.SemaphoreType.DMA(())],
        ),
        out_shape=jax.ShapeDtypeStruct((n_rows, d), F32),
        compiler_params=pltpu.CompilerParams(dimension_semantics=("arbitrary",), vmem_limit_bytes=VMEM_LIMIT),
        name="moe_dispatch",
    )(off, cnt, end, x2d, g_ffn, route_i)


def _moe_gmm_kernel(te_ref, xs_ref, wg_ref, wu_ref, wd_ref, ys_ref, *, chunk):
    del te_ref
    ys_ref[...] = _swiglu_part(xs_ref[...].astype(BF16), wg_ref, wu_ref, wd_ref, chunk)


def _moe_gmm(xs, tile_expert, wg, wu, wd, tg, chunk):
    n_rows, d = xs.shape
    _, _, ff = wg.shape
    row = lambda i, te: (i, 0)
    by_expert = lambda i, te: (te[i], 0, 0)
    return pl.pallas_call(
        functools.partial(_moe_gmm_kernel, chunk=chunk),
        grid_spec=pltpu.PrefetchScalarGridSpec(
            num_scalar_prefetch=1,
            grid=(n_rows // tg,),
            in_specs=[pl.BlockSpec((tg, d), row),
                      pl.BlockSpec((None, d, ff), by_expert),
                      pl.BlockSpec((None, d, ff), by_expert),
                      pl.BlockSpec((None, ff, d), by_expert)],
            out_specs=pl.BlockSpec((tg, d), row),
        ),
        out_shape=jax.ShapeDtypeStruct((n_rows, d), F32),
        compiler_params=pltpu.CompilerParams(dimension_semantics=("arbitrary",), vmem_limit_bytes=VMEM_LIMIT),
        name="moe_gmm",
    )(tile_expert, xs, wg, wu, wd)


def _moe_combine_kernel(off_ref, x_ref, rw_ref, ri_ref, ys_ref, y_ref, a_scr, b_scr, sem):
    tm = x_ref.shape[0]
    bufs = (a_scr, b_scr)

    def start(r, c):
        for k in range(TOP_K):
            src = off_ref[ri_ref[k, r]] + ri_ref[TOP_K + k, r]
            _row_copy(ys_ref, src, bufs[k], r, sem).start()
        return c

    def wait(r, c):
        for k in range(TOP_K):
            _row_copy(ys_ref, 0, bufs[k], 0, sem).wait()
        return c

    lax.fori_loop(0, tm, start, 0, unroll=DMA_UNROLL)
    lax.fori_loop(0, tm, wait, 0, unroll=DMA_UNROLL)
    rw = rw_ref[...]
    y_ref[...] = x_ref[...] + rw[:, 0:1] * a_scr[...] + rw[:, 1:2] * b_scr[...]


def _moe_combine(x2d, route_w, route_i, ys, off, tm):
    t, d = x2d.shape
    return pl.pallas_call(
        _moe_combine_kernel,
        grid_spec=pltpu.PrefetchScalarGridSpec(
            num_scalar_prefetch=1,
            grid=(t // tm,),
            in_specs=[pl.BlockSpec((tm, d), lambda i, *_: (i, 0)),
                      pl.BlockSpec((tm, LANES), lambda i, *_: (i, 0)),
                      pl.BlockSpec((None, ROUTE_ROWS, tm), lambda i, *_: (i, 0, 0), memory_space=pltpu.SMEM),
                      pl.BlockSpec(memory_space=pl.ANY)],
            out_specs=pl.BlockSpec((tm, d), lambda i, *_: (i, 0)),
            scratch_shapes=[pltpu.VMEM((tm, d), F32), pltpu.VMEM((tm, d), F32), pltpu.SemaphoreType.DMA(())],
        ),
        out_shape=jax.ShapeDtypeStruct((t, d), F32),
        compiler_params=pltpu.CompilerParams(dimension_semantics=("arbitrary",), vmem_limit_bytes=VMEM_LIMIT),
        name="moe_combine",
    )(off, x2d, route_w, route_i, ys)


def _moe_ffn(x2d, g_ffn, w_router, wg, wu, wd, tm, tg, chunk):
    t, _ = x2d.shape
    n_e = wg.shape[0]
    route_w, route_i, counts = _moe_route(x2d, g_ffn, w_router, tm)
    cnt = counts[0, :n_e]
    tiles = (cnt + tg - 1) // tg
    tile_end = jnp.cumsum(tiles)
    off = ((tile_end - tiles) * tg).astype(jnp.int32)
    n_tiles = (TOP_K * t) // tg + n_e
    tile_ids = jnp.arange(n_tiles, dtype=jnp.int32)
    tile_expert = jnp.sum((tile_ids[:, None] >= tile_end[None, :]).astype(jnp.int32), axis=1)
    tile_expert = jnp.minimum(tile_expert, n_e - 1).astype(jnp.int32)
    zero_end = (tile_end * tg).astype(jnp.int32).at[n_e - 1].set(n_tiles * tg)
    xs = _moe_dispatch(x2d, g_ffn, route_i, off, cnt, zero_end, n_tiles * tg, tm)
    ys = _moe_gmm(xs, tile_expert, wg, wu, wd, tg, chunk)
    return _moe_combine(x2d, route_w, route_i, ys, off, tm)


def _segment_matrix(bounds):
    seg_id = jnp.zeros((bounds[-1],), jnp.int32)
    for b in bounds[1:-1]:
        seg_id = seg_id + (jnp.arange(bounds[-1]) >= b).astype(jnp.int32)
    return (seg_id[:, None] == seg_id[None, :]).astype(BF16)


def _constants(seq):
    head_bounds = [0, HEAD_DIM, HEAD_DIM + ROPE_DIM, HEAD_PAD]
    pair_bounds = head_bounds + [HEAD_PAD + b for b in head_bounds[1:]]
    inv_head = jnp.concatenate([jnp.full((HEAD_DIM,), 1.0 / HEAD_DIM, F32),
                                jnp.full((HEAD_PAD - HEAD_DIM,), 1.0 / ROPE_DIM, F32)])
    half = ROPE_DIM // 2
    inv = 1.0 / (ROPE_THETA ** (jnp.arange(0, ROPE_DIM, 2, dtype=F32) / ROPE_DIM))
    ang = jnp.arange(seq, dtype=F32)[:, None] * inv[None, :]
    ones = jnp.ones((seq, HEAD_DIM), F32)
    zeros = jnp.zeros((seq, HEAD_DIM), F32)
    pad1 = jnp.ones((seq, HEAD_PAD - HEAD_DIM - ROPE_DIM), F32)
    cos = jnp.concatenate([ones, jnp.cos(ang), jnp.cos(ang), pad1], axis=-1)
    sin = jnp.concatenate([zeros, -jnp.sin(ang), jnp.sin(ang), 0.0 * pad1], axis=-1)
    assert cos.shape == (seq, HEAD_PAD) and half * 2 == ROPE_DIM
    return {
        "seg_mla": _segment_matrix(pair_bounds),
        "inv_mla": jnp.concatenate([inv_head, inv_head])[None, :],
        "seg64": _segment_matrix(list(range(0, PAIR + 1, HEAD_DIM))),
        "cos": cos,
        "sin": sin,
    }


def _layer_weights(l, p):
    d = p["w_in"].shape[1]
    q_rank = p["g_q_lat"].shape[1]
    kv_rank = p["g_kv_lat"].shape[1]
    qk_dim = HEAD_DIM + ROPE_DIM
    pad = HEAD_PAD - qk_dim
    w_in = p["w_in"][l]
    o = q_rank + kv_rank
    w_kpe = w_in[:, o:o + ROPE_DIM]
    w_in_p = jnp.concatenate([
        w_in[:, :o],
        jnp.zeros((d, HEAD_DIM), F32), w_kpe, jnp.zeros((d, pad), F32),
        w_in[:, o + ROPE_DIM:],
    ], axis=1).astype(BF16)

    w_q = p["w_q_up"][l].reshape(q_rank, N_MLA_HEADS, qk_dim)
    w_q = jnp.pad(w_q, ((0, 0), (0, 0), (0, pad))).reshape(q_rank, N_MLA_HEADS * HEAD_PAD).astype(BF16)
    w_kv = p["w_kv_up"][l].reshape(kv_rank, N_MLA_HEADS, 2 * HEAD_DIM)
    w_k = jnp.pad(w_kv[:, :, :HEAD_DIM], ((0, 0), (0, 0), (0, HEAD_PAD - HEAD_DIM)))
    w_k = w_k.reshape(kv_rank, N_MLA_HEADS * HEAD_PAD).astype(BF16)
    w_v = w_kv[:, :, HEAD_DIM:].reshape(kv_rank, N_MLA_HEADS * HEAD_DIM).astype(BF16)

    gq = jnp.pad(p["g_q_mla"][l], (0, pad))
    gk = jnp.pad(p["g_k_mla"][l][:HEAD_DIM], (0, HEAD_PAD - HEAD_DIM))
    gkpe = jnp.pad(p["g_k_mla"][l][HEAD_DIM:], (HEAD_DIM, pad))
    g_out = p["g_out"][l]
    mla_w = N_MLA_HEADS * HEAD_DIM
    mem_w = N_MEM_HEADS * HEAD_DIM
    return {
        "g_mix": p["g_mix"][l][None, :],
        "w_in": w_in_p,
        "g_q_lat": p["g_q_lat"][l][None, :],
        "w_q": w_q,
        "g_kv_lat": p["g_kv_lat"][l][None, :],
        "w_k": w_k,
        "w_v": w_v,
        "gq_row": jnp.tile(gq, N_MLA_HEADS)[None, :],
        "gk_row": jnp.tile(gk, N_MLA_HEADS)[None, :],
        "gkpe_row": gkpe[None, :],
        "g_mem": p["g_mem"][l][None, :],
        "w_mem_kv": p["w_mem_kv"][l].astype(BF16),
        "gqm_row": jnp.tile(p["g_q_mem"][l], N_MEM_HEADS)[None, :],
        "gkm_row": jnp.tile(p["g_k_mem"][l], N_MEM_HEADS)[None, :],
        "conv_w": p["conv_w"][l],
        "g_out_mla": g_out[None, :mla_w],
        "g_out_mem": g_out[None, mla_w:mla_w + mem_w],
        "g_out_conv": g_out[None, mla_w + mem_w:],
        "w_out": p["w_out"][l].astype(BF16),
        "g_ffn": p["g_ffn"][l][None, :],
    }


def _tile_config(seq):
    return dict(tm=min(512, seq), tq=min(512, seq), tm_ffn=512, ffn_chunk=1408, moe_chunk=1408)


def _forward(p, *, tm, tq, tm_ffn, ffn_chunk, moe_chunk):
    x = p["x"]
    batch, seq, d = x.shape
    depth = p["g_mix"].shape[0]
    assert seq % tm == 0 and seq % tq == 0 and (batch * seq) % tm_ffn == 0
    consts = _constants(seq)
    x2d = x.reshape(batch * seq, d)
    for l in range(depth):
        lw = _layer_weights(l, p)
        km, vm = _mem_kv(p["mem"], lw["g_mem"], lw["w_mem_kv"], consts["seg64"], lw["gkm_row"])
        q, k, v, o_mem, o_conv = _proj(x2d, seq, tm, lw, km, vm, consts["cos"], consts["sin"], consts)
        o_mla = _mla_attn(q, k, v, batch, seq, tq)
        x2d = _out_proj(x2d, o_mla, o_mem, o_conv, lw["g_out_mla"], lw["w_out"], tm)
        if l % 2 == 0:
            w_gu = p["w_dense_gu"][l // 2]
            ff = w_gu.shape[1] // 2
            x2d = _dense_ffn(x2d, lw["g_ffn"], w_gu[:, :ff].astype(BF16), w_gu[:, ff:].astype(BF16),
                             p["w_dense_down"][l // 2].astype(BF16), tm_ffn, ffn_chunk)
        else:
            w_gu = p["w_expert_gu"][l // 2]
            ff = w_gu.shape[2] // 2
            w_router = jnp.pad(p["w_router"][l // 2], ((0, 0), (0, LANES - N_EXPERTS)))
            x2d = _moe_ffn(x2d, lw["g_ffn"], w_router, w_gu[:, :, :ff].astype(BF16), w_gu[:, :, ff:].astype(BF16),
                           p["w_expert_down"][l // 2].astype(BF16), tm_ffn, tm_ffn, moe_chunk)
    return x2d.reshape(batch, seq, d)


def kernel(x, mem, g_mix, w_in, g_q_lat, w_q_up, g_kv_lat, w_kv_up, g_q_mla, g_k_mla, g_mem, w_mem_kv, g_q_mem,
           g_k_mem, conv_w, g_out, w_out, g_ffn, w_dense_gu, w_dense_down, w_router, w_expert_gu, w_expert_down):
    p = dict(x=x, mem=mem, g_mix=g_mix, w_in=w_in, g_q_lat=g_q_lat, w_q_up=w_q_up, g_kv_lat=g_kv_lat,
             w_kv_up=w_kv_up, g_q_mla=g_q_mla, g_k_mla=g_k_mla, g_mem=g_mem, w_mem_kv=w_mem_kv, g_q_mem=g_q_mem,
             g_k_mem=g_k_mem, conv_w=conv_w, g_out=g_out, w_out=w_out, g_ffn=g_ffn, w_dense_gu=w_dense_gu,
             w_dense_down=w_dense_down, w_router=w_router, w_expert_gu=w_expert_gu, w_expert_down=w_expert_down)
    return _forward(p, **_tile_config(x.shape[1]))
```

```python
import functools
import math

import jax
import jax.numpy as jnp
from jax import lax
from jax.experimental import pallas as pl
from jax.experimental.pallas import tpu as pltpu

F32 = jnp.float32
BF16 = jnp.bfloat16

EPS = 1e-6
ROPE_THETA = 10000.0
HEAD_DIM = 64
ROPE_DIM = 32
N_MLA_HEADS = 8
N_MEM_HEADS = 4
CONV_K = 3
N_EXPERTS = 8
LANES = 128
HEAD_PAD = LANES
PAIR = 2 * LANES
CARRY_ROWS = 8
VMEM_LIMIT = 56 * 1024 * 1024

NT_DIMS = (((1,), (1,)), ((), ()))


def _rms(x, g):
    return x * lax.rsqrt(jnp.mean(x * x, axis=-1, keepdims=True) + EPS) * g


def _seg_sumsq(x, seg):
    w = seg.shape[0]
    x2 = (x * x).astype(BF16)
    parts = [jnp.dot(x2[:, c:c + w], seg, preferred_element_type=F32) for c in range(0, x.shape[1], w)]
    return parts[0] if len(parts) == 1 else jnp.concatenate(parts, axis=-1)


def _rope_head(y, cos, sin_signed, lane):
    up = pltpu.roll(y, LANES - ROPE_DIM // 2, 1)
    down = pltpu.roll(y, ROPE_DIM // 2, 1)
    swapped = jnp.where(lane < HEAD_DIM + ROPE_DIM // 2, up, down)
    return y * cos + swapped * sin_signed


def _mem_kv_kernel(mem_ref, g_mem_ref, w_ref, seg_ref, gk_ref, km_ref, vm_ref):
    width = km_ref.shape[-1]
    mn = _rms(mem_ref[...], g_mem_ref[...]).astype(BF16)
    kv = jnp.dot(mn, w_ref[...], preferred_element_type=F32)
    km = kv[:, :width]
    ss = _seg_sumsq(km, seg_ref[...])
    km_ref[...] = (km * lax.rsqrt(ss * (1.0 / HEAD_DIM) + EPS) * gk_ref[...]).astype(BF16)
    vm_ref[...] = kv[:, width:].astype(BF16)


def _mem_kv(mem, g_mem, w_mem_kv, seg64, gk_row):
    b, m, d = mem.shape
    width = w_mem_kv.shape[1] // 2
    const = lambda i: (0, 0)
    return pl.pallas_call(
        _mem_kv_kernel,
        grid=(b,),
        in_specs=[
            pl.BlockSpec((None, m, d), lambda i: (i, 0, 0)),
            pl.BlockSpec((1, d), const),
            pl.BlockSpec((d, 2 * width), const),
            pl.BlockSpec((PAIR, PAIR), const),
            pl.BlockSpec((1, width), const),
        ],
        out_specs=[
            pl.BlockSpec((None, m, width), lambda i: (i, 0, 0)),
            pl.BlockSpec((None, m, width), lambda i: (i, 0, 0)),
        ],
        out_shape=[jax.ShapeDtypeStruct((b, m, width), BF16)] * 2,
        compiler_params=pltpu.CompilerParams(dimension_semantics=("parallel",), vmem_limit_bytes=VMEM_LIMIT),
        name="mem_kv",
    )(mem, g_mem, w_mem_kv, seg64, gk_row)


def _proj_kernel(x_ref, g_mix_ref, w_in_ref, g_qlat_ref, w_q_ref, g_kvlat_ref, w_k_ref, w_v_ref,
                 seg_mla_ref, inv_mla_ref, gq_ref, gk_ref, gkpe_ref, cos_ref, sin_ref,
                 km_ref, vm_ref, seg64_ref, gqm_ref, conv_w_ref, g_omem_ref, g_oconv_ref,
                 q_ref, k_ref, v_ref, omem_ref, oconv_ref, cu_ref, *, tiles_per_seq, q_scale):
    tm = x_ref.shape[0]
    q_rank = g_qlat_ref.shape[1]
    kv_rank = g_kvlat_ref.shape[1]
    mem_w = gqm_ref.shape[1]
    conv_c = conv_w_ref.shape[1]

    h = _rms(x_ref[...], g_mix_ref[...]).astype(BF16)
    z = jnp.dot(h, w_in_ref[...], preferred_element_type=F32)
    o = 0
    q_lat = z[:, o:o + q_rank]; o += q_rank
    kv_lat = z[:, o:o + kv_rank]; o += kv_rank
    kpe = z[:, o:o + HEAD_PAD]; o += HEAD_PAD
    q_mem = z[:, o:o + mem_w]; o += mem_w
    gate_b = z[:, o:o + conv_c]; o += conv_c
    gate_c = z[:, o:o + conv_c]; o += conv_c
    u = z[:, o:o + conv_c]

    lane = lax.broadcasted_iota(jnp.int32, (tm, LANES), 1)
    cos = cos_ref[...]
    sin = sin_ref[...]
    seg_mla = seg_mla_ref[...]
    inv_mla = inv_mla_ref[...]

    qn = _rms(q_lat, g_qlat_ref[...]).astype(BF16)
    qf = jnp.dot(qn, w_q_ref[...], preferred_element_type=F32)
    n_rep = qf.shape[1] // PAIR
    inv_row = jnp.concatenate([inv_mla] * n_rep, axis=-1)
    qy = qf * lax.rsqrt(_seg_sumsq(qf, seg_mla) * inv_row + EPS) * gq_ref[...]
    for hd in range(qf.shape[1] // HEAD_PAD):
        sl = slice(hd * HEAD_PAD, (hd + 1) * HEAD_PAD)
        q_ref[:, sl] = (_rope_head(qy[:, sl], cos, sin, lane) * q_scale).astype(BF16)

    kvn = _rms(kv_lat, g_kvlat_ref[...]).astype(BF16)
    kf = jnp.dot(kvn, w_k_ref[...], preferred_element_type=F32)
    vt = lax.dot_general(w_v_ref[...], kvn, NT_DIMS, preferred_element_type=F32)
    tk = v_ref.shape[-1]
    for c in range(tm // tk):
        v_ref[c] = vt[:, c * tk:(c + 1) * tk].astype(BF16)
    ky = kf * lax.rsqrt(_seg_sumsq(kf, seg_mla) * inv_row + EPS) * gk_ref[...]
    kpe_y = kpe * lax.rsqrt(_seg_sumsq(kpe, seg_mla[:LANES, :LANES]) * inv_mla[:, :LANES] + EPS) * gkpe_ref[...]
    kpe_r = _rope_head(kpe_y, cos, sin, lane)
    for hd in range(kf.shape[1] // HEAD_PAD):
        sl = slice(hd * HEAD_PAD, (hd + 1) * HEAD_PAD)
        k_ref[:, sl] = (ky[:, sl] + kpe_r).astype(BF16)

    qm = q_mem * lax.rsqrt(_seg_sumsq(q_mem, seg64_ref[...]) * (1.0 / HEAD_DIM) + EPS) * gqm_ref[...]
    km = km_ref[...]
    vm = vm_ref[...]
    lane_m = lax.broadcasted_iota(jnp.int32, (km.shape[0], LANES), 1)
    groups = []
    for grp in range(mem_w // LANES):
        sl = slice(grp * LANES, (grp + 1) * LANES)
        qg, kg, vg = qm[:, sl], km[:, sl], vm[:, sl]
        og = None
        for half in range(LANES // HEAD_DIM):
            in_head = (lane >= half * HEAD_DIM) & (lane < (half + 1) * HEAD_DIM)
            in_head_m = (lane_m >= half * HEAD_DIM) & (lane_m < (half + 1) * HEAD_DIM)
            qh = jnp.where(in_head, qg, 0.0).astype(BF16)
            s = lax.dot_general(qh, kg, NT_DIMS, preferred_element_type=F32) * (1.0 / math.sqrt(HEAD_DIM))
            p = jnp.exp(s - jnp.max(s, axis=-1, keepdims=True))
            p = (p * (1.0 / jnp.sum(p, axis=-1, keepdims=True))).astype(BF16)
            vh = jnp.where(in_head_m, vg, jnp.zeros_like(vg))
            oh = jnp.dot(p, vh, preferred_element_type=F32)
            og = oh if og is None else og + oh
        groups.append(og)
    o_mem = jnp.concatenate(groups, axis=-1)
    omem_ref[...] = _rms(o_mem, g_omem_ref[...]).astype(BF16)

    @pl.when(pl.program_id(0) % tiles_per_seq == 0)
    def _():
        cu_ref[0:CARRY_ROWS, :] = jnp.zeros((CARRY_ROWS, conv_c), F32)

    cu = gate_c * u
    cu_ref[CARRY_ROWS:CARRY_ROWS + tm, :] = cu
    w = conv_w_ref[...]
    conv = w[CONV_K - 1:CONV_K] * cu
    for tap in range(CONV_K - 1):
        shift = CONV_K - 1 - tap
        conv = conv + w[tap:tap + 1] * cu_ref[CARRY_ROWS - shift:CARRY_ROWS - shift + tm, :]
    oconv_ref[...] = _rms(gate_b * conv, g_oconv_ref[...]).astype(BF16)
    cu_ref[0:CARRY_ROWS, :] = cu_ref[tm:tm + CARRY_ROWS, :]


def _proj(x2d, seq, tm, tk, lw, km, vm, cos, sin, consts):
    t, d = x2d.shape
    tiles_per_seq = seq // tm
    n_q = lw["w_q"].shape[1]
    n_v = lw["w_v_t"].shape[0]
    mem_w = km.shape[-1]
    conv_c = lw["conv_w"].shape[1]
    const = lambda i: (0, 0)
    row = lambda i: (i, 0)
    full = lambda a: pl.BlockSpec(a.shape, const)
    operands = [
        (x2d, pl.BlockSpec((tm, d), row)),
        (lw["g_mix"], None), (lw["w_in"], None), (lw["g_q_lat"], None), (lw["w_q"], None),
        (lw["g_kv_lat"], None), (lw["w_k"], None), (lw["w_v_t"], None),
        (consts["seg_mla"], None), (consts["inv_mla"], None),
        (lw["gq_row"], None), (lw["gk_row"], None), (lw["gkpe_row"], None),
        (cos, pl.BlockSpec((tm, LANES), lambda i: (i % tiles_per_seq, 0))),
        (sin, pl.BlockSpec((tm, LANES), lambda i: (i % tiles_per_seq, 0))),
        (km, pl.BlockSpec((None,) + km.shape[1:], lambda i: (i // tiles_per_seq, 0, 0))),
        (vm, pl.BlockSpec((None,) + vm.shape[1:], lambda i: (i // tiles_per_seq, 0, 0))),
        (consts["seg64"], None), (lw["gqm_row"], None), (lw["conv_w"], None),
        (lw["g_out_mem"], None), (lw["g_out_conv"], None),
    ]
    args = [a for a, _ in operands]
    specs = [full(a) if s is None else s for a, s in operands]
    out_widths = (n_q, n_q, mem_w, conv_c)
    row_out = [(pl.BlockSpec((tm, w), row), jax.ShapeDtypeStruct((t, w), BF16)) for w in out_widths]
    vt_out = (pl.BlockSpec((tm // tk, n_v, tk), lambda i: (i, 0, 0)), jax.ShapeDtypeStruct((t // tk, n_v, tk), BF16))
    outs = row_out[:2] + [vt_out] + row_out[2:]
    return pl.pallas_call(
        functools.partial(_proj_kernel, tiles_per_seq=tiles_per_seq,
                          q_scale=math.log2(math.e) / math.sqrt(HEAD_DIM + ROPE_DIM)),
        grid=(t // tm,),
        in_specs=specs,
        out_specs=[spec for spec, _ in outs],
        out_shape=[shape for _, shape in outs],
        scratch_shapes=[pltpu.VMEM((tm + CARRY_ROWS, conv_c), F32)],
        compiler_params=pltpu.CompilerParams(dimension_semantics=("arbitrary",), vmem_limit_bytes=VMEM_LIMIT),
        name="proj",
    )(*args)


def _mla_attn_kernel(q_ref, k_ref, vt_ref, o_ref, s_scr):
    tq = q_ref.shape[0]
    tk = vt_ref.shape[-1]
    assert tq == 2 * tk and s_scr.shape[0] == 2
    qi = pl.program_id(2)
    neg = jnp.finfo(F32).min
    k_pos = lax.broadcasted_iota(jnp.int32, (tk, tq), 0)
    q_pos = lax.broadcasted_iota(jnp.int32, (tk, tq), 1) + qi * tq

    n_heads = PAIR // HEAD_PAD
    head_lanes = [slice(hh * HEAD_PAD, (hh + 1) * HEAD_PAD) for hh in range(n_heads)]
    head_rows = [slice(hh * HEAD_DIM, (hh + 1) * HEAD_DIM) for hh in range(n_heads)]
    qs = [q_ref[:, sl] for sl in head_lanes]

    def scores_into(slot, j):
        start = pl.multiple_of(j * tk, tk)
        for hh in range(n_heads):
            kb = k_ref[pl.ds(start, tk), head_lanes[hh]]
            s_scr[slot, hh] = lax.dot_general(kb, qs[hh], NT_DIMS, preferred_element_type=F32)

    def consume(slot, j, carry, masked):
        vt = vt_ref[j]
        new = []
        for hh in range(n_heads):
            m, l, acc = carry[hh]
            s = s_scr[slot, hh]
            if masked:
                s = jnp.where(k_pos + j * tk <= q_pos, s, neg)
            m_new = jnp.maximum(m, jnp.max(s, axis=0, keepdims=True))
            alpha = jnp.exp2(m - m_new)
            p = jnp.exp2(s - m_new)
            l = alpha * l + jnp.sum(p, axis=0, keepdims=True)
            acc = alpha * acc + jnp.dot(vt[head_rows[hh], :], p.astype(BF16), preferred_element_type=F32)
            new.append((m_new, l, acc))
        return tuple(new)

    def pair(i, carry, masked):
        scores_into(1, 2 * i + 1)
        carry = consume(0, 2 * i, carry, masked)
        if not masked:
            scores_into(0, 2 * i + 2)
        return consume(1, 2 * i + 1, carry, masked)

    init = tuple((jnp.full((1, tq), -jnp.inf, F32), jnp.zeros((1, tq), F32), jnp.zeros((HEAD_DIM, tq), F32))
                 for _ in range(n_heads))
    scores_into(0, 0)
    carry = lax.fori_loop(0, qi, functools.partial(pair, masked=False), init)
    carry = pair(qi, carry, True)
    out_t = jnp.concatenate([acc * (1.0 / l) for _, l, acc in carry], axis=0)
    o_ref[...] = out_t.T.astype(BF16)


def _mla_attn(q, k, vt, batch, seq, tq):
    t = q.shape[0]
    n_pairs = q.shape[1] // PAIR
    nq = seq // tq
    _, v_rows, tk = vt.shape
    nk = seq // tk
    return pl.pallas_call(
        _mla_attn_kernel,
        grid=(batch, n_pairs, nq),
        in_specs=[
            pl.BlockSpec((tq, PAIR), lambda b, hp, i: (b * nq + i, hp)),
            pl.BlockSpec((seq, PAIR), lambda b, hp, i: (b, hp)),
            pl.BlockSpec((nk, v_rows // n_pairs, tk), lambda b, hp, i: (b, hp, 0)),
        ],
        out_specs=pl.BlockSpec((tq, LANES), lambda b, hp, i: (b * nq + i, hp)),
        out_shape=jax.ShapeDtypeStruct((t, n_pairs * LANES), BF16),
        scratch_shapes=[pltpu.VMEM((2, PAIR // HEAD_PAD, tk, tq), F32)],
        compiler_params=pltpu.CompilerParams(dimension_semantics=("parallel", "parallel", "parallel"),
                                             vmem_limit_bytes=VMEM_LIMIT),
        name="mla_attn",
    )(q, k, vt)


def _out_proj_kernel(x_ref, omla_ref, omem_ref, oconv_ref, g_ref, w_ref, y_ref):
    a = _rms(omla_ref[...].astype(F32), g_ref[...]).astype(BF16)
    cat = jnp.concatenate([a, omem_ref[...], oconv_ref[...]], axis=-1)
    y_ref[...] = x_ref[...] + jnp.dot(cat, w_ref[...], preferred_element_type=F32)


def _out_proj(x2d, o_mla, o_mem, o_conv, g_out_mla, w_out, tm):
    t, d = x2d.shape
    row = lambda i: (i, 0)
    const = lambda i: (0, 0)
    return pl.pallas_call(
        _out_proj_kernel,
        grid=(t // tm,),
        in_specs=[
            pl.BlockSpec((tm, d), row),
            pl.BlockSpec((tm, o_mla.shape[1]), row),
            pl.BlockSpec((tm, o_mem.shape[1]), row),
            pl.BlockSpec((tm, o_conv.shape[1]), row),
            pl.BlockSpec(g_out_mla.shape, const),
            pl.BlockSpec(w_out.shape, const),
        ],
        out_specs=pl.BlockSpec((tm, d), row),
        out_shape=jax.ShapeDtypeStruct((t, d), F32),
        compiler_params=pltpu.CompilerParams(dimension_semantics=("parallel",), vmem_limit_bytes=VMEM_LIMIT),
        name="out_proj",
    )(x2d, o_mla, o_mem, o_conv, g_out_mla, w_out)


def _swiglu_part(h, wg_ref, wu_ref, wd_ref, chunk):
    ff = wg_ref.shape[-1]
    out = None
    for c in range(0, ff, chunk):
        g = jnp.dot(h, wg_ref[:, c:c + chunk], preferred_element_type=F32)
        u = jnp.dot(h, wu_ref[:, c:c + chunk], preferred_element_type=F32)
        act = (g * (1.0 / (1.0 + jnp.exp(-g))) * u).astype(BF16)
        part = jnp.dot(act, wd_ref[c:c + chunk, :], preferred_element_type=F32)
        out = part if out is None else out + part
    return out


def _dense_ffn_kernel(x_ref, g_ref, wg_ref, wu_ref, wd_ref, y_ref, *, chunk):
    x = x_ref[...]
    h = _rms(x, g_ref[...]).astype(BF16)
    y_ref[...] = x + _swiglu_part(h, wg_ref, wu_ref, wd_ref, chunk)


def _dense_ffn(x2d, g_ffn, wg, wu, wd, tm, chunk):
    t, d = x2d.shape
    row = lambda i: (i, 0)
    const = lambda i: (0, 0)
    resident = lambda a: pl.BlockSpec(a.shape, const, pipeline_mode=pl.Buffered(1))
    return pl.pallas_call(
        functools.partial(_dense_ffn_kernel, chunk=chunk),
        grid=(t // tm,),
        in_specs=[pl.BlockSpec((tm, d), row), pl.BlockSpec(g_ffn.shape, const),
                  resident(wg), resident(wu), resident(wd)],
        out_specs=pl.BlockSpec((tm, d), row),
        out_shape=jax.ShapeDtypeStruct((t, d), F32),
        compiler_params=pltpu.CompilerParams(dimension_semantics=("parallel",), vmem_limit_bytes=VMEM_LIMIT),
        name="dense_ffn",
    )(x2d, g_ffn, wg, wu, wd)


ROUTE_ROWS = 8
TOP_K = 2
DMA_UNROLL = 8
ROW_TILE = 8


def _moe_route_kernel(x_ref, g_ref, wr_ref, tri_ref, rw_ref, ri_ref, cnt_ref, base_scr):
    tm = x_ref.shape[0]
    lane = lax.broadcasted_iota(jnp.int32, (tm, LANES), 1)
    lane_f = lane.astype(F32)

    @pl.when(pl.program_id(0) == 0)
    def _():
        base_scr[...] = jnp.zeros_like(base_scr)

    hf = _rms(x_ref[...], g_ref[...])
    wr = wr_ref[...]
    h_hi = hf.astype(BF16)
    h_lo = (hf - h_hi.astype(F32)).astype(BF16)
    w_hi = wr.astype(BF16)
    w_lo = (wr - w_hi.astype(F32)).astype(BF16)
    logits = (jnp.dot(h_hi, w_hi, preferred_element_type=F32) + jnp.dot(h_hi, w_lo, preferred_element_type=F32)
              + jnp.dot(h_lo, w_hi, preferred_element_type=F32))
    logits = jnp.where(lane < N_EXPERTS, logits, -jnp.inf)
    m1 = jnp.max(logits, axis=-1, keepdims=True)
    i1 = jnp.min(jnp.where(logits == m1, lane_f, float(LANES)), axis=-1, keepdims=True)
    rest = jnp.where(lane_f == i1, -jnp.inf, logits)
    m2 = jnp.max(rest, axis=-1, keepdims=True)
    i2 = jnp.min(jnp.where(rest == m2, lane_f, float(LANES)), axis=-1, keepdims=True)
    e2 = jnp.exp(m2 - m1)
    w1 = 1.0 / (1.0 + e2)
    rw_ref[...] = jnp.where(lane == 0, w1, jnp.where(lane == 1, e2 * w1, 0.0))

    chosen = jnp.where((lane_f == i1) | (lane_f == i2), 1.0, 0.0)
    before = jnp.dot(tri_ref[...], chosen.astype(BF16), preferred_element_type=F32) + base_scr[...]
    r1 = jnp.sum(jnp.where(lane_f == i1, before, 0.0), axis=-1, keepdims=True)
    r2 = jnp.sum(jnp.where(lane_f == i2, before, 0.0), axis=-1, keepdims=True)
    base_scr[...] += jnp.sum(chosen, axis=0, keepdims=True)
    cnt_ref[...] = base_scr[...].astype(jnp.int32)
    table = jnp.where(lane == 0, i1, jnp.where(lane == 1, i2, jnp.where(lane == 2, r1, jnp.where(lane == 3, r2, 0.0))))
    ri_ref[...] = table.T[:ROUTE_ROWS, :].astype(jnp.int32)


def _moe_route(x2d, g_ffn, w_router, tm):
    t, d = x2d.shape
    tri = (jnp.arange(tm)[:, None] > jnp.arange(tm)[None, :]).astype(BF16)
    const = lambda i: (0, 0)
    return pl.pallas_call(
        _moe_route_kernel,
        grid=(t // tm,),
        in_specs=[pl.BlockSpec((tm, d), lambda i: (i, 0)), pl.BlockSpec(g_ffn.shape, const),
                  pl.BlockSpec(w_router.shape, const), pl.BlockSpec((tm, tm), const)],
        out_specs=[pl.BlockSpec((tm, LANES), lambda i: (i, 0)),
                   pl.BlockSpec((None, ROUTE_ROWS, tm), lambda i: (i, 0, 0)),
                   pl.BlockSpec((1, LANES), const)],
        out_shape=[jax.ShapeDtypeStruct((t, LANES), F32),
                   jax.ShapeDtypeStruct((t // tm, ROUTE_ROWS, tm), jnp.int32),
                   jax.ShapeDtypeStruct((1, LANES), jnp.int32)],
        scratch_shapes=[pltpu.VMEM((1, LANES), F32)],
        compiler_params=pltpu.CompilerParams(dimension_semantics=("arbitrary",), vmem_limit_bytes=VMEM_LIMIT),
        name="moe_route",
    )(x2d, g_ffn, w_router, tri)


def _token_copy(src_ref, src_tok, dst_ref, dst_tok, sem):
    src = src_ref.at[pl.ds(pl.multiple_of(src_tok * ROW_TILE, ROW_TILE), ROW_TILE)]
    dst = dst_ref.at[pl.ds(pl.multiple_of(dst_tok * ROW_TILE, ROW_TILE), ROW_TILE)]
    return pltpu.make_async_copy(src, dst, sem)


def _to_token_tiles(ref, value):
    rows = value.shape[0]
    for c in range(ROW_TILE):
        ref[pl.ds(c, rows, stride=ROW_TILE), :] = value[:, c * LANES:(c + 1) * LANES]


def _from_token_tiles(ref, rows):
    return jnp.concatenate([ref[pl.ds(c, rows, stride=ROW_TILE), :] for c in range(ROW_TILE)], axis=-1)


def _moe_dispatch_kernel(zlo_ref, zhi_ref, x_ref, g_ref, pos_ref, xs_ref, h_scr, zero_scr, sem):
    tm = x_ref.shape[0]
    _to_token_tiles(h_scr, _rms(x_ref[...], g_ref[...]))

    def start(r, c):
        for k in range(TOP_K):
            _token_copy(h_scr, r, xs_ref, pos_ref[0, k * tm + r], sem).start(priority=k)
        return c

    def wait(r, c):
        for k in range(TOP_K):
            _token_copy(h_scr, 0, xs_ref, 0, sem).wait()
        return c

    lax.fori_loop(0, tm, start, 0, unroll=DMA_UNROLL)
    lax.fori_loop(0, tm, wait, 0, unroll=DMA_UNROLL)

    @pl.when(pl.program_id(0) == pl.num_programs(0) - 1)
    def _():
        zero_scr[...] = jnp.zeros_like(zero_scr)
        for e in range(N_EXPERTS):
            def zstart(r, c):
                _token_copy(zero_scr, 0, xs_ref, r, sem).start()
                return c

            def zwait(r, c):
                _token_copy(zero_scr, 0, xs_ref, 0, sem).wait()
                return c

            lax.fori_loop(zlo_ref[e], zhi_ref[e], zstart, 0)
            lax.fori_loop(zlo_ref[e], zhi_ref[e], zwait, 0)


def _moe_dispatch(x2d, g_ffn, pos, zero_lo, zero_hi, n_rows, tm):
    t, d = x2d.shape
    assert d == ROW_TILE * LANES
    return pl.pallas_call(
        _moe_dispatch_kernel,
        grid_spec=pltpu.PrefetchScalarGridSpec(
            num_scalar_prefetch=2,
            grid=(t // tm,),
            in_specs=[pl.BlockSpec((tm, d), lambda i, *_: (i, 0)),
                      pl.BlockSpec(g_ffn.shape, lambda i, *_: (0, 0)),
                      pl.BlockSpec((None, 1, TOP_K * tm), lambda i, *_: (i, 0, 0), memory_space=pltpu.SMEM)],
            out_specs=pl.BlockSpec(memory_space=pl.ANY),
            scratch_shapes=[pltpu.VMEM((tm * ROW_TILE, LANES), F32), pltpu.VMEM((ROW_TILE, LANES), F32),
                            pltpu.SemaphoreType.DMA(())],
        ),
        out_shape=jax.ShapeDtypeStruct((n_rows * ROW_TILE, LANES), F32),
        compiler_params=pltpu.CompilerParams(dimension_semantics=("arbitrary",), vmem_limit_bytes=VMEM_LIMIT),
        name="moe_dispatch",
    )(zero_lo, zero_hi, x2d, g_ffn, pos)


def _moe_gmm_kernel(te_ref, xs_ref, wg_ref, wu_ref, wd_ref, ys_ref, *, chunk):
    del te_ref
    tg = xs_ref.shape[0] // ROW_TILE
    x = _from_token_tiles(xs_ref, tg).astype(BF16)
    _to_token_tiles(ys_ref, _swiglu_part(x, wg_ref, wu_ref, wd_ref, chunk))


def _moe_gmm(xs, tile_expert, wg, wu, wd, tg, chunk):
    _, d, ff = wg.shape
    n_tiles = xs.shape[0] // (tg * ROW_TILE)
    row = lambda i, te: (i, 0)
    by_expert = lambda i, te: (te[i], 0, 0)
    return pl.pallas_call(
        functools.partial(_moe_gmm_kernel, chunk=chunk),
        grid_spec=pltpu.PrefetchScalarGridSpec(
            num_scalar_prefetch=1,
            grid=(n_tiles,),
            in_specs=[pl.BlockSpec((tg * ROW_TILE, LANES), row),
                      pl.BlockSpec((None, d, ff), by_expert),
                      pl.BlockSpec((None, d, ff), by_expert),
                      pl.BlockSpec((None, ff, d), by_expert)],
            out_specs=pl.BlockSpec((tg * ROW_TILE, LANES), row),
        ),
        out_shape=jax.ShapeDtypeStruct(xs.shape, F32),
        compiler_params=pltpu.CompilerParams(dimension_semantics=("arbitrary",), vmem_limit_bytes=VMEM_LIMIT),
        name="moe_gmm",
    )(tile_expert, xs, wg, wu, wd)


def _moe_combine_kernel(x_ref, rw_ref, pos_ref, ys_ref, y_ref, a_scr, b_scr, sem):
    tm = x_ref.shape[0]
    bufs = (a_scr, b_scr)

    def start(r, c):
        for k in range(TOP_K):
            _token_copy(ys_ref, pos_ref[0, k * tm + r], bufs[k], r, sem).start(priority=k)
        return c

    def wait(r, c):
        for k in range(TOP_K):
            _token_copy(ys_ref, 0, bufs[k], 0, sem).wait()
        return c

    lax.fori_loop(0, tm, start, 0, unroll=DMA_UNROLL)
    lax.fori_loop(0, tm, wait, 0, unroll=DMA_UNROLL)
    rw = rw_ref[...]
    y_ref[...] = x_ref[...] + rw[:, 0:1] * _from_token_tiles(a_scr, tm) + rw[:, 1:2] * _from_token_tiles(b_scr, tm)


def _moe_combine(x2d, route_w, pos, ys, tm):
    t, d = x2d.shape
    return pl.pallas_call(
        _moe_combine_kernel,
        grid=(t // tm,),
        in_specs=[pl.BlockSpec((tm, d), lambda i: (i, 0)),
                  pl.BlockSpec((tm, LANES), lambda i: (i, 0)),
                  pl.BlockSpec((None, 1, TOP_K * tm), lambda i: (i, 0, 0), memory_space=pltpu.SMEM),
                  pl.BlockSpec(memory_space=pl.ANY)],
        out_specs=pl.BlockSpec((tm, d), lambda i: (i, 0)),
        out_shape=jax.ShapeDtypeStruct((t, d), F32),
        scratch_shapes=[pltpu.VMEM((tm * ROW_TILE, LANES), F32), pltpu.VMEM((tm * ROW_TILE, LANES), F32),
                        pltpu.SemaphoreType.DMA(())],
        compiler_params=pltpu.CompilerParams(dimension_semantics=("arbitrary",), vmem_limit_bytes=VMEM_LIMIT),
        name="moe_combine",
    )(x2d, route_w, pos, ys)


def _moe_ffn(x2d, g_ffn, w_router, wg, wu, wd, tm, tg, chunk):
    t, _ = x2d.shape
    n_e = wg.shape[0]
    route_w, route_i, counts = _moe_route(x2d, g_ffn, w_router, tm)
    cnt = counts[0, :n_e]
    tiles = (cnt + tg - 1) // tg
    tile_end = jnp.cumsum(tiles)
    off = ((tile_end - tiles) * tg).astype(jnp.int32)
    n_tiles = (TOP_K * t) // tg + n_e
    tile_ids = jnp.arange(n_tiles, dtype=jnp.int32)
    tile_expert = jnp.sum((tile_ids[:, None] >= tile_end[None, :]).astype(jnp.int32), axis=1)
    tile_expert = jnp.minimum(tile_expert, n_e - 1).astype(jnp.int32)
    zero_hi = (tile_end * tg).astype(jnp.int32).at[n_e - 1].set(n_tiles * tg)
    pos = jnp.take(off, route_i[:, :TOP_K, :]) + route_i[:, TOP_K:2 * TOP_K, :]
    pos = pos.reshape(t // tm, 1, TOP_K * tm)
    xs = _moe_dispatch(x2d, g_ffn, pos, off + cnt, zero_hi, n_tiles * tg, tm)
    ys = _moe_gmm(xs, tile_expert, wg, wu, wd, tg, chunk)
    return _moe_combine(x2d, route_w, pos, ys, tm)


def _segment_matrix(bounds):
    seg_id = jnp.zeros((bounds[-1],), jnp.int32)
    for b in bounds[1:-1]:
        seg_id = seg_id + (jnp.arange(bounds[-1]) >= b).astype(jnp.int32)
    return (seg_id[:, None] == seg_id[None, :]).astype(BF16)


def _constants(seq):
    head_bounds = [0, HEAD_DIM, HEAD_DIM + ROPE_DIM, HEAD_PAD]
    pair_bounds = head_bounds + [HEAD_PAD + b for b in head_bounds[1:]]
    inv_head = jnp.concatenate([jnp.full((HEAD_DIM,), 1.0 / HEAD_DIM, F32),
                                jnp.full((HEAD_PAD - HEAD_DIM,), 1.0 / ROPE_DIM, F32)])
    half = ROPE_DIM // 2
    inv = 1.0 / (ROPE_THETA ** (jnp.arange(0, ROPE_DIM, 2, dtype=F32) / ROPE_DIM))
    ang = jnp.arange(seq, dtype=F32)[:, None] * inv[None, :]
    ones = jnp.ones((seq, HEAD_DIM), F32)
    zeros = jnp.zeros((seq, HEAD_DIM), F32)
    pad1 = jnp.ones((seq, HEAD_PAD - HEAD_DIM - ROPE_DIM), F32)
    cos = jnp.concatenate([ones, jnp.cos(ang), jnp.cos(ang), pad1], axis=-1)
    sin = jnp.concatenate([zeros, -jnp.sin(ang), jnp.sin(ang), 0.0 * pad1], axis=-1)
    assert cos.shape == (seq, HEAD_PAD) and half * 2 == ROPE_DIM
    return {
        "seg_mla": _segment_matrix(pair_bounds),
        "inv_mla": jnp.concatenate([inv_head, inv_head])[None, :],
        "seg64": _segment_matrix(list(range(0, PAIR + 1, HEAD_DIM))),
        "cos": cos,
        "sin": sin,
    }


def _layer_weights(l, p):
    d = p["w_in"].shape[1]
    q_rank = p["g_q_lat"].shape[1]
    kv_rank = p["g_kv_lat"].shape[1]
    qk_dim = HEAD_DIM + ROPE_DIM
    pad = HEAD_PAD - qk_dim
    w_in = p["w_in"][l]
    o = q_rank + kv_rank
    w_kpe = w_in[:, o:o + ROPE_DIM]
    w_in_p = jnp.concatenate([
        w_in[:, :o],
        jnp.zeros((d, HEAD_DIM), F32), w_kpe, jnp.zeros((d, pad), F32),
        w_in[:, o + ROPE_DIM:],
    ], axis=1).astype(BF16)

    w_q = p["w_q_up"][l].reshape(q_rank, N_MLA_HEADS, qk_dim)
    w_q = jnp.pad(w_q, ((0, 0), (0, 0), (0, pad))).reshape(q_rank, N_MLA_HEADS * HEAD_PAD).astype(BF16)
    w_kv = p["w_kv_up"][l].reshape(kv_rank, N_MLA_HEADS, 2 * HEAD_DIM)
    w_k = jnp.pad(w_kv[:, :, :HEAD_DIM], ((0, 0), (0, 0), (0, HEAD_PAD - HEAD_DIM)))
    w_k = w_k.reshape(kv_rank, N_MLA_HEADS * HEAD_PAD).astype(BF16)
    w_v = w_kv[:, :, HEAD_DIM:].reshape(kv_rank, N_MLA_HEADS * HEAD_DIM).astype(BF16)

    gq = jnp.pad(p["g_q_mla"][l], (0, pad))
    gk = jnp.pad(p["g_k_mla"][l][:HEAD_DIM], (0, HEAD_PAD - HEAD_DIM))
    gkpe = jnp.pad(p["g_k_mla"][l][HEAD_DIM:], (HEAD_DIM, pad))
    g_out = p["g_out"][l]
    mla_w = N_MLA_HEADS * HEAD_DIM
    mem_w = N_MEM_HEADS * HEAD_DIM
    return {
        "g_mix": p["g_mix"][l][None, :],
        "w_in": w_in_p,
        "g_q_lat": p["g_q_lat"][l][None, :],
        "w_q": w_q,
        "g_kv_lat": p["g_kv_lat"][l][None, :],
        "w_k": w_k,
        "w_v_t": w_v.T,
        "gq_row": jnp.tile(gq, N_MLA_HEADS)[None, :],
        "gk_row": jnp.tile(gk, N_MLA_HEADS)[None, :],
        "gkpe_row": gkpe[None, :],
        "g_mem": p["g_mem"][l][None, :],
        "w_mem_kv": p["w_mem_kv"][l].astype(BF16),
        "gqm_row": jnp.tile(p["g_q_mem"][l], N_MEM_HEADS)[None, :],
        "gkm_row": jnp.tile(p["g_k_mem"][l], N_MEM_HEADS)[None, :],
        "conv_w": p["conv_w"][l],
        "g_out_mla": g_out[None, :mla_w],
        "g_out_mem": g_out[None, mla_w:mla_w + mem_w],
        "g_out_conv": g_out[None, mla_w + mem_w:],
        "w_out": p["w_out"][l].astype(BF16),
        "g_ffn": p["g_ffn"][l][None, :],
    }


def _tile_config(seq):
    return dict(tm=min(512, seq), tq=min(512, seq), tk=min(256, seq), tm_ffn=512, ffn_chunk=1408, moe_chunk=1408)


def _forward(p, *, tm, tq, tk, tm_ffn, ffn_chunk, moe_chunk):
    x = p["x"]
    batch, seq, d = x.shape
    depth = p["g_mix"].shape[0]
    assert seq % tm == 0 and seq % tq == 0 and (batch * seq) % tm_ffn == 0
    consts = _constants(seq)
    x2d = x.reshape(batch * seq, d)
    for l in range(depth):
        lw = _layer_weights(l, p)
        km, vm = _mem_kv(p["mem"], lw["g_mem"], lw["w_mem_kv"], consts["seg64"], lw["gkm_row"])
        q, k, vt, o_mem, o_conv = _proj(x2d, seq, tm, tk, lw, km, vm, consts["cos"], consts["sin"], consts)
        o_mla = _mla_attn(q, k, vt, batch, seq, tq)
        x2d = _out_proj(x2d, o_mla, o_mem, o_conv, lw["g_out_mla"], lw["w_out"], tm)
        if l % 2 == 0:
            w_gu = p["w_dense_gu"][l // 2]
            ff = w_gu.shape[1] // 2
            x2d = _dense_ffn(x2d, lw["g_ffn"], w_gu[:, :ff].astype(BF16), w_gu[:, ff:].astype(BF16),
                             p["w_dense_down"][l // 2].astype(BF16), tm_ffn, ffn_chunk)
        else:
            w_gu = p["w_expert_gu"][l // 2]
            ff = w_gu.shape[2] // 2
            w_router = jnp.pad(p["w_router"][l // 2], ((0, 0), (0, LANES - N_EXPERTS)))
            x2d = _moe_ffn(x2d, lw["g_ffn"], w_router, w_gu[:, :, :ff].astype(BF16), w_gu[:, :, ff:].astype(BF16),
                           p["w_expert_down"][l // 2].astype(BF16), tm_ffn, tm_ffn, moe_chunk)
    return x2d.reshape(batch, seq, d)


def kernel(x, mem, g_mix, w_in, g_q_lat, w_q_up, g_kv_lat, w_kv_up, g_q_mla, g_k_mla, g_mem, w_mem_kv, g_q_mem,
           g_k_mem, conv_w, g_out, w_out, g_ffn, w_dense_gu, w_dense_down, w_router, w_expert_gu, w_expert_down):
    p = dict(x=x, mem=mem, g_mix=g_mix, w_in=w_in, g_q_lat=g_q_lat, w_q_up=w_q_up, g_kv_lat=g_kv_lat,
             w_kv_up=w_kv_up, g_q_mla=g_q_mla, g_k_mla=g_k_mla, g_mem=g_mem, w_mem_kv=w_mem_kv, g_q_mem=g_q_mem,
             g_k_mem=g_k_mem, conv_w=conv_w, g_out=g_out, w_out=w_out, g_ffn=g_ffn, w_dense_gu=w_dense_gu,
             w_dense_down=w_dense_down, w_router=w_router, w_expert_gu=w_expert_gu, w_expert_down=w_expert_down)
    return _forward(p, **_tile_config(x.shape[1]))
```

```python
import functools
import math

import jax
import jax.numpy as jnp
from jax import lax
from jax.experimental import pallas as pl
from jax.experimental.pallas import tpu as pltpu

F32 = jnp.float32
BF16 = jnp.bfloat16

EPS = 1e-6
ROPE_THETA = 10000.0
HEAD_DIM = 64
ROPE_DIM = 32
N_MLA_HEADS = 8
N_MEM_HEADS = 4
CONV_K = 3
N_EXPERTS = 8
LANES = 128
HEAD_PAD = LANES
PAIR = 2 * LANES
CARRY_ROWS = 8
SUM_ROWS = 16
V_AUG = HEAD_DIM + SUM_ROWS
VMEM_LIMIT = 56 * 1024 * 1024

NT_DIMS = (((1,), (1,)), ((), ()))
Q_SCALE = math.log2(math.e) / math.sqrt(HEAD_DIM + ROPE_DIM)


def _rms(x, g):
    return x * lax.rsqrt(jnp.mean(x * x, axis=-1, keepdims=True) + EPS) * g


def _seg_sumsq(x, seg):
    w = seg.shape[0]
    x2 = (x * x).astype(BF16)
    parts = [jnp.dot(x2[:, c:c + w], seg, preferred_element_type=F32) for c in range(0, x.shape[1], w)]
    return parts[0] if len(parts) == 1 else jnp.concatenate(parts, axis=-1)


def _rope_head(y, cos, sin_signed, lane):
    up = pltpu.roll(y, LANES - ROPE_DIM // 2, 1)
    down = pltpu.roll(y, ROPE_DIM // 2, 1)
    swapped = jnp.where(lane < HEAD_DIM + ROPE_DIM // 2, up, down)
    return y * cos + swapped * sin_signed


def _mem_kv_kernel(mem_ref, g_mem_ref, w_ref, seg_ref, gk_ref, km_ref, vm_ref):
    width = km_ref.shape[-1]
    mn = _rms(mem_ref[...], g_mem_ref[...]).astype(BF16)
    kv = jnp.dot(mn, w_ref[...], preferred_element_type=F32)
    km = kv[:, :width]
    ss = _seg_sumsq(km, seg_ref[...])
    km_ref[...] = (km * lax.rsqrt(ss * (1.0 / HEAD_DIM) + EPS) * gk_ref[...]).astype(BF16)
    vm_ref[...] = kv[:, width:].astype(BF16)


def _mem_kv(mem, g_mem, w_mem_kv, seg64, gk_row):
    b, m, d = mem.shape
    width = w_mem_kv.shape[1] // 2
    const = lambda i: (0, 0)
    return pl.pallas_call(
        _mem_kv_kernel,
        grid=(b,),
        in_specs=[
            pl.BlockSpec((None, m, d), lambda i: (i, 0, 0)),
            pl.BlockSpec((1, d), const),
            pl.BlockSpec((d, 2 * width), const),
            pl.BlockSpec((PAIR, PAIR), const),
            pl.BlockSpec((1, width), const),
        ],
        out_specs=[
            pl.BlockSpec((None, m, width), lambda i: (i, 0, 0)),
            pl.BlockSpec((None, m, width), lambda i: (i, 0, 0)),
        ],
        out_shape=[jax.ShapeDtypeStruct((b, m, width), BF16)] * 2,
        compiler_params=pltpu.CompilerParams(dimension_semantics=("parallel",), vmem_limit_bytes=VMEM_LIMIT),
        name="mem_kv",
    )(mem, g_mem, w_mem_kv, seg64, gk_row)


def _proj_kernel(x_ref, g_mix_ref, w_in_ref, g_qlat_ref, w_q_ref, g_kvlat_ref, w_k_ref, w_v_ref, v_ones_ref,
                 seg_mla_ref, inv_mla_ref, gq_ref, gk_ref, gkpe_ref, cos_ref, sin_ref,
                 km_ref, vm_ref, seg64_ref, gqm_ref, conv_w_ref, g_omem_ref, g_oconv_ref,
                 q_ref, k_ref, v_ref, omem_ref, oconv_ref, cu_ref, *, tiles_per_seq):
    tm = x_ref.shape[0]
    q_rank = g_qlat_ref.shape[1]
    kv_rank = g_kvlat_ref.shape[1]
    mem_w = gqm_ref.shape[1]
    conv_c = conv_w_ref.shape[1]

    h = _rms(x_ref[...], g_mix_ref[...]).astype(BF16)
    z = jnp.dot(h, w_in_ref[...], preferred_element_type=F32)
    o = 0
    q_lat = z[:, o:o + q_rank]; o += q_rank
    kv_lat = z[:, o:o + kv_rank]; o += kv_rank
    kpe = z[:, o:o + HEAD_PAD]; o += HEAD_PAD
    q_mem = z[:, o:o + mem_w]; o += mem_w
    gate_b = z[:, o:o + conv_c]; o += conv_c
    gate_c = z[:, o:o + conv_c]; o += conv_c
    u = z[:, o:o + conv_c]

    lane = lax.broadcasted_iota(jnp.int32, (tm, LANES), 1)
    cos = cos_ref[...]
    sin = sin_ref[...]
    seg_mla = seg_mla_ref[...]
    inv_mla = inv_mla_ref[...]

    qn = _rms(q_lat, g_qlat_ref[...]).astype(BF16)
    qf = jnp.dot(qn, w_q_ref[...], preferred_element_type=F32)
    n_rep = qf.shape[1] // PAIR
    inv_row = jnp.concatenate([inv_mla] * n_rep, axis=-1)
    qy = qf * lax.rsqrt(_seg_sumsq(qf, seg_mla) * inv_row + EPS) * gq_ref[...]
    for hd in range(qf.shape[1] // HEAD_PAD):
        sl = slice(hd * HEAD_PAD, (hd + 1) * HEAD_PAD)
        q_ref[:, sl] = _rope_head(qy[:, sl], cos, sin, lane).astype(BF16)

    kvn = _rms(kv_lat, g_kvlat_ref[...]).astype(BF16)
    kf = jnp.dot(kvn, w_k_ref[...], preferred_element_type=F32)
    vt = lax.dot_general(w_v_ref[...], kvn, NT_DIMS, preferred_element_type=F32)
    tk = v_ref.shape[-1]
    ones_rows = jnp.concatenate([v_ones_ref[...]] * (tk // LANES), axis=1)
    for c in range(tm // tk):
        v_ref[c] = (vt[:, c * tk:(c + 1) * tk] + ones_rows).astype(BF16)
    ky = kf * lax.rsqrt(_seg_sumsq(kf, seg_mla) * inv_row + EPS) * gk_ref[...]
    kpe_y = kpe * lax.rsqrt(_seg_sumsq(kpe, seg_mla[:LANES, :LANES]) * inv_mla[:, :LANES] + EPS) * gkpe_ref[...]
    kpe_r = _rope_head(kpe_y, cos, sin, lane)
    for hd in range(kf.shape[1] // HEAD_PAD):
        sl = slice(hd * HEAD_PAD, (hd + 1) * HEAD_PAD)
        k_ref[:, sl] = (ky[:, sl] + kpe_r).astype(BF16)

    qm = q_mem * lax.rsqrt(_seg_sumsq(q_mem, seg64_ref[...]) * (1.0 / HEAD_DIM) + EPS) * gqm_ref[...]
    km = km_ref[...]
    vm = vm_ref[...]
    lane_m = lax.broadcasted_iota(jnp.int32, (km.shape[0], LANES), 1)
    groups = []
    for grp in range(mem_w // LANES):
        sl = slice(grp * LANES, (grp + 1) * LANES)
        qg, kg, vg = qm[:, sl], km[:, sl], vm[:, sl]
        og = None
        for half in range(LANES // HEAD_DIM):
            in_head = (lane >= half * HEAD_DIM) & (lane < (half + 1) * HEAD_DIM)
            in_head_m = (lane_m >= half * HEAD_DIM) & (lane_m < (half + 1) * HEAD_DIM)
            qh = jnp.where(in_head, qg, 0.0).astype(BF16)
            s = lax.dot_general(qh, kg, NT_DIMS, preferred_element_type=F32) * (1.0 / math.sqrt(HEAD_DIM))
            p = jnp.exp(s - jnp.max(s, axis=-1, keepdims=True))
            p = (p * (1.0 / jnp.sum(p, axis=-1, keepdims=True))).astype(BF16)
            vh = jnp.where(in_head_m, vg, jnp.zeros_like(vg))
            oh = jnp.dot(p, vh, preferred_element_type=F32)
            og = oh if og is None else og + oh
        groups.append(og)
    o_mem = jnp.concatenate(groups, axis=-1)
    omem_ref[...] = _rms(o_mem, g_omem_ref[...]).astype(BF16)

    @pl.when(pl.program_id(0) % tiles_per_seq == 0)
    def _():
        cu_ref[0:CARRY_ROWS, :] = jnp.zeros((CARRY_ROWS, conv_c), F32)

    cu = gate_c * u
    cu_ref[CARRY_ROWS:CARRY_ROWS + tm, :] = cu
    w = conv_w_ref[...]
    conv = w[CONV_K - 1:CONV_K] * cu
    for tap in range(CONV_K - 1):
        shift = CONV_K - 1 - tap
        conv = conv + w[tap:tap + 1] * cu_ref[CARRY_ROWS - shift:CARRY_ROWS - shift + tm, :]
    oconv_ref[...] = _rms(gate_b * conv, g_oconv_ref[...]).astype(BF16)
    cu_ref[0:CARRY_ROWS, :] = cu_ref[tm:tm + CARRY_ROWS, :]


def _proj(x2d, seq, tm, tk, lw, km, vm, cos, sin, consts):
    t, d = x2d.shape
    tiles_per_seq = seq // tm
    n_q = lw["w_q"].shape[1]
    n_v = lw["w_v_t"].shape[0]
    mem_w = km.shape[-1]
    conv_c = lw["conv_w"].shape[1]
    const = lambda i: (0, 0)
    row = lambda i: (i, 0)
    full = lambda a: pl.BlockSpec(a.shape, const)
    operands = [
        (x2d, pl.BlockSpec((tm, d), row)),
        (lw["g_mix"], None), (lw["w_in"], None), (lw["g_q_lat"], None), (lw["w_q"], None),
        (lw["g_kv_lat"], None), (lw["w_k"], None), (lw["w_v_t"], None), (consts["v_ones"], None),
        (consts["seg_mla"], None), (consts["inv_mla"], None),
        (lw["gq_row"], None), (lw["gk_row"], None), (lw["gkpe_row"], None),
        (cos, pl.BlockSpec((tm, LANES), lambda i: (i % tiles_per_seq, 0))),
        (sin, pl.BlockSpec((tm, LANES), lambda i: (i % tiles_per_seq, 0))),
        (km, pl.BlockSpec((None,) + km.shape[1:], lambda i: (i // tiles_per_seq, 0, 0))),
        (vm, pl.BlockSpec((None,) + vm.shape[1:], lambda i: (i // tiles_per_seq, 0, 0))),
        (consts["seg64"], None), (lw["gqm_row"], None), (lw["conv_w"], None),
        (lw["g_out_mem"], None), (lw["g_out_conv"], None),
    ]
    args = [a for a, _ in operands]
    specs = [full(a) if s is None else s for a, s in operands]
    out_widths = (n_q, n_q, mem_w, conv_c)
    row_out = [(pl.BlockSpec((tm, w), row), jax.ShapeDtypeStruct((t, w), BF16)) for w in out_widths]
    vt_out = (pl.BlockSpec((tm // tk, n_v, tk), lambda i: (i, 0, 0)), jax.ShapeDtypeStruct((t // tk, n_v, tk), BF16))
    outs = row_out[:2] + [vt_out] + row_out[2:]
    return pl.pallas_call(
        functools.partial(_proj_kernel, tiles_per_seq=tiles_per_seq),
        grid=(t // tm,),
        in_specs=specs,
        out_specs=[spec for spec, _ in outs],
        out_shape=[shape for _, shape in outs],
        scratch_shapes=[pltpu.VMEM((tm + CARRY_ROWS, conv_c), F32)],
        compiler_params=pltpu.CompilerParams(dimension_semantics=("arbitrary",), vmem_limit_bytes=VMEM_LIMIT),
        name="proj",
    )(*args)


def _mla_attn_kernel(q_ref, k_ref, vt_ref, o_ref, s_scr):
    tq = q_ref.shape[0]
    tk = vt_ref.shape[-1]
    assert tq == 2 * tk and s_scr.shape[0] == 2
    qi = pl.program_id(2)
    neg = jnp.finfo(F32).min
    visible = lax.broadcasted_iota(jnp.int32, (tk, tk), 0) <= lax.broadcasted_iota(jnp.int32, (tk, tk), 1)

    n_heads = PAIR // HEAD_PAD
    head_lanes = [slice(hh * HEAD_PAD, (hh + 1) * HEAD_PAD) for hh in range(n_heads)]
    head_rows = [slice(hh * V_AUG, (hh + 1) * V_AUG) for hh in range(n_heads)]
    qs = [q_ref[:, sl] for sl in head_lanes]

    def key_block(j, hh):
        return k_ref[pl.ds(pl.multiple_of(j * tk, tk), tk), head_lanes[hh]]

    def scores_into(slot, j):
        for hh in range(n_heads):
            s_scr[slot, hh] = lax.dot_general(key_block(j, hh), qs[hh], NT_DIMS, preferred_element_type=F32)

    def update(state, s, vt_h):
        m, acc = state
        m_new = jnp.maximum(m, jnp.max(s, axis=0, keepdims=True))
        p = jnp.exp2(s - m_new).astype(BF16)
        return m_new, jnp.exp2(m - m_new) * acc + jnp.dot(vt_h, p, preferred_element_type=F32)

    def consume(slot, j, carry):
        vt = vt_ref[j]
        return tuple(update(carry[hh], s_scr[slot, hh], vt[head_rows[hh], :]) for hh in range(n_heads))

    def pair(i, carry):
        scores_into(1, 2 * i + 1)
        carry = consume(0, 2 * i, carry)
        scores_into(0, 2 * i + 2)
        return consume(1, 2 * i + 1, carry)

    init = tuple((jnp.full((1, tq), -jnp.inf, F32), jnp.zeros((V_AUG, tq), F32)) for _ in range(n_heads))
    scores_into(0, 0)
    carry = lax.fori_loop(0, qi, pair, init)

    vt_a = vt_ref[2 * qi]
    vt_b = vt_ref[2 * qi + 1]
    outs = []
    for hh in range(n_heads):
        m, acc = carry[hh]
        s_a = s_scr[0, hh]
        s_b = lax.dot_general(key_block(2 * qi + 1, hh), qs[hh][tk:, :], NT_DIMS, preferred_element_type=F32)
        va, vb = vt_a[head_rows[hh], :], vt_b[head_rows[hh], :]
        _, acc_l = update((m[:, :tk], acc[:, :tk]), jnp.where(visible, s_a[:, :tk], neg), va)
        right = update((m[:, tk:], acc[:, tk:]), s_a[:, tk:], va)
        _, acc_r = update(right, jnp.where(visible, s_b, neg), vb)
        acc = jnp.concatenate([acc_l, acc_r], axis=1)
        outs.append(acc[:HEAD_DIM] * (1.0 / acc[HEAD_DIM:HEAD_DIM + 1]))
    o_ref[...] = jnp.concatenate(outs, axis=0).T.astype(BF16)


def _mla_attn(q, k, vt, batch, seq, tq):
    t = q.shape[0]
    n_pairs = q.shape[1] // PAIR
    nq = seq // tq
    _, v_rows, tk = vt.shape
    nk = seq // tk
    return pl.pallas_call(
        _mla_attn_kernel,
        grid=(batch, n_pairs, nq),
        in_specs=[
            pl.BlockSpec((tq, PAIR), lambda b, hp, i: (b * nq + i, hp)),
            pl.BlockSpec((seq, PAIR), lambda b, hp, i: (b, hp)),
            pl.BlockSpec((nk, v_rows // n_pairs, tk), lambda b, hp, i: (b, hp, 0)),
        ],
        out_specs=pl.BlockSpec((tq, LANES), lambda b, hp, i: (b * nq + i, hp)),
        out_shape=jax.ShapeDtypeStruct((t, n_pairs * LANES), BF16),
        scratch_shapes=[pltpu.VMEM((2, PAIR // HEAD_PAD, tk, tq), F32)],
        compiler_params=pltpu.CompilerParams(dimension_semantics=("parallel", "parallel", "parallel"),
                                             vmem_limit_bytes=VMEM_LIMIT),
        name="mla_attn",
    )(q, k, vt)


N_MIX_OPERANDS = 6


def _mixer_residual(x_ref, omla_ref, omem_ref, oconv_ref, g_ref, w_ref):
    a = _rms(omla_ref[...].astype(F32), g_ref[...]).astype(BF16)
    cat = jnp.concatenate([a, omem_ref[...], oconv_ref[...]], axis=-1)
    return x_ref[...] + jnp.dot(cat, w_ref[...], preferred_element_type=F32)


def _mixer_specs(mix, tm):
    row = lambda i: (i, 0)
    const = lambda i: (0, 0)
    return ([pl.BlockSpec((tm, a.shape[1]), row) for a in mix[:4]]
            + [pl.BlockSpec(mix[4].shape, const), pl.BlockSpec(mix[5].shape, const, pipeline_mode=pl.Buffered(1))])


def _out_proj_kernel(*refs):
    refs[N_MIX_OPERANDS][...] = _mixer_residual(*refs[:N_MIX_OPERANDS])


def _out_proj(mix, tm):
    t, d = mix[0].shape
    return pl.pallas_call(
        _out_proj_kernel,
        grid=(t // tm,),
        in_specs=_mixer_specs(mix, tm),
        out_specs=pl.BlockSpec((tm, d), lambda i: (i, 0)),
        out_shape=jax.ShapeDtypeStruct((t, d), F32),
        compiler_params=pltpu.CompilerParams(dimension_semantics=("parallel",), vmem_limit_bytes=VMEM_LIMIT),
        name="out_proj",
    )(*mix)


def _swiglu_part(h, wg_ref, wu_ref, wd_ref, chunk):
    ff = wg_ref.shape[-1]
    out = None
    for c in range(0, ff, chunk):
        g = jnp.dot(h, wg_ref[:, c:c + chunk], preferred_element_type=F32)
        u = jnp.dot(h, wu_ref[:, c:c + chunk], preferred_element_type=F32)
        act = (g * (1.0 / (1.0 + jnp.exp(-g))) * u).astype(BF16)
        part = jnp.dot(act, wd_ref[c:c + chunk, :], preferred_element_type=F32)
        out = part if out is None else out + part
    return out


def _dense_ffn_kernel(*refs, chunk):
    mix_refs, (g_ref, wg_ref, wu_ref, wd_ref, y_ref) = refs[:N_MIX_OPERANDS], refs[N_MIX_OPERANDS:]
    x = _mixer_residual(*mix_refs)
    h = _rms(x, g_ref[...]).astype(BF16)
    y_ref[...] = x + _swiglu_part(h, wg_ref, wu_ref, wd_ref, chunk)


def _dense_ffn(mix, g_ffn, wg, wu, wd, tm, chunk):
    t, d = mix[0].shape
    row = lambda i: (i, 0)
    const = lambda i: (0, 0)
    resident = lambda a: pl.BlockSpec(a.shape, const, pipeline_mode=pl.Buffered(1))
    return pl.pallas_call(
        functools.partial(_dense_ffn_kernel, chunk=chunk),
        grid=(t // tm,),
        in_specs=_mixer_specs(mix, tm) + [pl.BlockSpec(g_ffn.shape, const), resident(wg), resident(wu), resident(wd)],
        out_specs=pl.BlockSpec((tm, d), row),
        out_shape=jax.ShapeDtypeStruct((t, d), F32),
        compiler_params=pltpu.CompilerParams(dimension_semantics=("parallel",), vmem_limit_bytes=VMEM_LIMIT),
        name="dense_ffn",
    )(*mix, g_ffn, wg, wu, wd)


ROUTE_ROWS = 8
TOP_K = 2
DMA_UNROLL = 8
ROW_TILE = 8


def _moe_route_kernel(x_ref, g_ref, wr_ref, tri_ref, rw_ref, ri_ref, cnt_ref, base_scr):
    tm = x_ref.shape[0]
    lane = lax.broadcasted_iota(jnp.int32, (tm, LANES), 1)
    lane_f = lane.astype(F32)

    @pl.when(pl.program_id(0) == 0)
    def _():
        base_scr[...] = jnp.zeros_like(base_scr)

    hf = _rms(x_ref[...], g_ref[...])
    wr = wr_ref[...]
    h_hi = hf.astype(BF16)
    h_lo = (hf - h_hi.astype(F32)).astype(BF16)
    w_hi = wr.astype(BF16)
    w_lo = (wr - w_hi.astype(F32)).astype(BF16)
    logits = (jnp.dot(h_hi, w_hi, preferred_element_type=F32) + jnp.dot(h_hi, w_lo, preferred_element_type=F32)
              + jnp.dot(h_lo, w_hi, preferred_element_type=F32))
    logits = jnp.where(lane < N_EXPERTS, logits, -jnp.inf)
    m1 = jnp.max(logits, axis=-1, keepdims=True)
    i1 = jnp.min(jnp.where(logits == m1, lane_f, float(LANES)), axis=-1, keepdims=True)
    rest = jnp.where(lane_f == i1, -jnp.inf, logits)
    m2 = jnp.max(rest, axis=-1, keepdims=True)
    i2 = jnp.min(jnp.where(rest == m2, lane_f, float(LANES)), axis=-1, keepdims=True)
    e2 = jnp.exp(m2 - m1)
    w1 = 1.0 / (1.0 + e2)
    rw_ref[...] = jnp.where(lane == 0, w1, jnp.where(lane == 1, e2 * w1, 0.0))

    chosen = jnp.where((lane_f == i1) | (lane_f == i2), 1.0, 0.0)
    before = jnp.dot(tri_ref[...], chosen.astype(BF16), preferred_element_type=F32) + base_scr[...]
    r1 = jnp.sum(jnp.where(lane_f == i1, before, 0.0), axis=-1, keepdims=True)
    r2 = jnp.sum(jnp.where(lane_f == i2, before, 0.0), axis=-1, keepdims=True)
    base_scr[...] += jnp.sum(chosen, axis=0, keepdims=True)
    cnt_ref[...] = base_scr[...].astype(jnp.int32)
    table = jnp.where(lane == 0, i1, jnp.where(lane == 1, i2, jnp.where(lane == 2, r1, jnp.where(lane == 3, r2, 0.0))))
    ri_ref[...] = table.T[:ROUTE_ROWS, :].astype(jnp.int32)


def _moe_route(x2d, g_ffn, w_router, tm):
    t, d = x2d.shape
    tri = (jnp.arange(tm)[:, None] > jnp.arange(tm)[None, :]).astype(BF16)
    const = lambda i: (0, 0)
    return pl.pallas_call(
        _moe_route_kernel,
        grid=(t // tm,),
        in_specs=[pl.BlockSpec((tm, d), lambda i: (i, 0)), pl.BlockSpec(g_ffn.shape, const),
                  pl.BlockSpec(w_router.shape, const), pl.BlockSpec((tm, tm), const)],
        out_specs=[pl.BlockSpec((tm, LANES), lambda i: (i, 0)),
                   pl.BlockSpec((None, ROUTE_ROWS, tm), lambda i: (i, 0, 0)),
                   pl.BlockSpec((1, LANES), const)],
        out_shape=[jax.ShapeDtypeStruct((t, LANES), F32),
                   jax.ShapeDtypeStruct((t // tm, ROUTE_ROWS, tm), jnp.int32),
                   jax.ShapeDtypeStruct((1, LANES), jnp.int32)],
        scratch_shapes=[pltpu.VMEM((1, LANES), F32)],
        compiler_params=pltpu.CompilerParams(dimension_semantics=("arbitrary",), vmem_limit_bytes=VMEM_LIMIT),
        name="moe_route",
    )(x2d, g_ffn, w_router, tri)


def _token_copy(src_ref, src_tok, dst_ref, dst_tok, sem):
    src = src_ref.at[pl.ds(pl.multiple_of(src_tok * ROW_TILE, ROW_TILE), ROW_TILE)]
    dst = dst_ref.at[pl.ds(pl.multiple_of(dst_tok * ROW_TILE, ROW_TILE), ROW_TILE)]
    return pltpu.make_async_copy(src, dst, sem)


def _to_token_tiles(ref, value):
    rows = value.shape[0]
    for c in range(ROW_TILE):
        ref[pl.ds(c, rows, stride=ROW_TILE), :] = value[:, c * LANES:(c + 1) * LANES]


def _from_token_tiles(ref, rows):
    return jnp.concatenate([ref[pl.ds(c, rows, stride=ROW_TILE), :] for c in range(ROW_TILE)], axis=-1)


def _moe_dispatch_kernel(zlo_ref, zhi_ref, x_ref, g_ref, pos_ref, xs_ref, h_scr, zero_scr, sem):
    tm = x_ref.shape[0]
    _to_token_tiles(h_scr, _rms(x_ref[...], g_ref[...]))

    def start(r, c):
        for k in range(TOP_K):
            _token_copy(h_scr, r, xs_ref, pos_ref[0, k * tm + r], sem).start(priority=k)
        return c

    def wait(r, c):
        for k in range(TOP_K):
            _token_copy(h_scr, 0, xs_ref, 0, sem).wait()
        return c

    lax.fori_loop(0, tm, start, 0, unroll=DMA_UNROLL)
    lax.fori_loop(0, tm, wait, 0, unroll=DMA_UNROLL)

    @pl.when(pl.program_id(0) == pl.num_programs(0) - 1)
    def _():
        zero_scr[...] = jnp.zeros_like(zero_scr)
        for e in range(N_EXPERTS):
            def zstart(r, c):
                _token_copy(zero_scr, 0, xs_ref, r, sem).start()
                return c

            def zwait(r, c):
                _token_copy(zero_scr, 0, xs_ref, 0, sem).wait()
                return c

            lax.fori_loop(zlo_ref[e], zhi_ref[e], zstart, 0)
            lax.fori_loop(zlo_ref[e], zhi_ref[e], zwait, 0)


def _moe_dispatch(x2d, g_ffn, pos, zero_lo, zero_hi, n_rows, tm):
    t, d = x2d.shape
    assert d == ROW_TILE * LANES
    return pl.pallas_call(
        _moe_dispatch_kernel,
        grid_spec=pltpu.PrefetchScalarGridSpec(
            num_scalar_prefetch=2,
            grid=(t // tm,),
            in_specs=[pl.BlockSpec((tm, d), lambda i, *_: (i, 0)),
                      pl.BlockSpec(g_ffn.shape, lambda i, *_: (0, 0)),
                      pl.BlockSpec((None, 1, TOP_K * tm), lambda i, *_: (i, 0, 0), memory_space=pltpu.SMEM)],
            out_specs=pl.BlockSpec(memory_space=pl.ANY),
            scratch_shapes=[pltpu.VMEM((tm * ROW_TILE, LANES), F32), pltpu.VMEM((ROW_TILE, LANES), F32),
                            pltpu.SemaphoreType.DMA(())],
        ),
        out_shape=jax.ShapeDtypeStruct((n_rows * ROW_TILE, LANES), F32),
        compiler_params=pltpu.CompilerParams(dimension_semantics=("arbitrary",), vmem_limit_bytes=VMEM_LIMIT),
        name="moe_dispatch",
    )(zero_lo, zero_hi, x2d, g_ffn, pos)


def _moe_gmm_kernel(te_ref, xs_ref, wg_ref, wu_ref, wd_ref, ys_ref, *, chunk):
    del te_ref
    tg = xs_ref.shape[0] // ROW_TILE
    x = _from_token_tiles(xs_ref, tg).astype(BF16)
    _to_token_tiles(ys_ref, _swiglu_part(x, wg_ref, wu_ref, wd_ref, chunk))


def _moe_gmm(xs, tile_expert, wg, wu, wd, tg, chunk):
    _, d, ff = wg.shape
    n_tiles = xs.shape[0] // (tg * ROW_TILE)
    row = lambda i, te: (i, 0)
    by_expert = lambda i, te: (te[i], 0, 0)
    return pl.pallas_call(
        functools.partial(_moe_gmm_kernel, chunk=chunk),
        grid_spec=pltpu.PrefetchScalarGridSpec(
            num_scalar_prefetch=1,
            grid=(n_tiles,),
            in_specs=[pl.BlockSpec((tg * ROW_TILE, LANES), row),
                      pl.BlockSpec((None, d, ff), by_expert),
                      pl.BlockSpec((None, d, ff), by_expert),
                      pl.BlockSpec((None, ff, d), by_expert)],
            out_specs=pl.BlockSpec((tg * ROW_TILE, LANES), row),
        ),
        out_shape=jax.ShapeDtypeStruct(xs.shape, F32),
        compiler_params=pltpu.CompilerParams(dimension_semantics=("arbitrary",), vmem_limit_bytes=VMEM_LIMIT),
        name="moe_gmm",
    )(tile_expert, xs, wg, wu, wd)


def _moe_combine_kernel(x_ref, rw_ref, pos_ref, ys_ref, y_ref, a_scr, b_scr, sem):
    tm = x_ref.shape[0]
    bufs = (a_scr, b_scr)

    def start(r, c):
        for k in range(TOP_K):
            _token_copy(ys_ref, pos_ref[0, k * tm + r], bufs[k], r, sem).start(priority=k)
        return c

    def wait(r, c):
        for k in range(TOP_K):
            _token_copy(ys_ref, 0, bufs[k], 0, sem).wait()
        return c

    lax.fori_loop(0, tm, start, 0, unroll=DMA_UNROLL)
    lax.fori_loop(0, tm, wait, 0, unroll=DMA_UNROLL)
    rw = rw_ref[...]
    y_ref[...] = x_ref[...] + rw[:, 0:1] * _from_token_tiles(a_scr, tm) + rw[:, 1:2] * _from_token_tiles(b_scr, tm)


def _moe_combine(x2d, route_w, pos, ys, tm):
    t, d = x2d.shape
    return pl.pallas_call(
        _moe_combine_kernel,
        grid=(t // tm,),
        in_specs=[pl.BlockSpec((tm, d), lambda i: (i, 0)),
                  pl.BlockSpec((tm, LANES), lambda i: (i, 0)),
                  pl.BlockSpec((None, 1, TOP_K * tm), lambda i: (i, 0, 0), memory_space=pltpu.SMEM),
                  pl.BlockSpec(memory_space=pl.ANY)],
        out_specs=pl.BlockSpec((tm, d), lambda i: (i, 0)),
        out_shape=jax.ShapeDtypeStruct((t, d), F32),
        scratch_shapes=[pltpu.VMEM((tm * ROW_TILE, LANES), F32), pltpu.VMEM((tm * ROW_TILE, LANES), F32),
                        pltpu.SemaphoreType.DMA(())],
        compiler_params=pltpu.CompilerParams(dimension_semantics=("arbitrary",), vmem_limit_bytes=VMEM_LIMIT),
        name="moe_combine",
    )(x2d, route_w, pos, ys)


def _moe_ffn(x2d, g_ffn, w_router, wg, wu, wd, tm, tg, chunk):
    t, _ = x2d.shape
    n_e = wg.shape[0]
    route_w, route_i, counts = _moe_route(x2d, g_ffn, w_router, tm)
    cnt = counts[0, :n_e]
    tiles = (cnt + tg - 1) // tg
    tile_end = jnp.cumsum(tiles)
    off = ((tile_end - tiles) * tg).astype(jnp.int32)
    n_tiles = (TOP_K * t) // tg + n_e
    tile_ids = jnp.arange(n_tiles, dtype=jnp.int32)
    tile_expert = jnp.sum((tile_ids[:, None] >= tile_end[None, :]).astype(jnp.int32), axis=1)
    tile_expert = jnp.minimum(tile_expert, n_e - 1).astype(jnp.int32)
    zero_hi = (tile_end * tg).astype(jnp.int32).at[n_e - 1].set(n_tiles * tg)
    pos = jnp.take(off, route_i[:, :TOP_K, :]) + route_i[:, TOP_K:2 * TOP_K, :]
    pos = pos.reshape(t // tm, 1, TOP_K * tm)
    xs = _moe_dispatch(x2d, g_ffn, pos, off + cnt, zero_hi, n_tiles * tg, tm)
    ys = _moe_gmm(xs, tile_expert, wg, wu, wd, tg, chunk)
    return _moe_combine(x2d, route_w, pos, ys, tm)


def _segment_matrix(bounds):
    seg_id = jnp.zeros((bounds[-1],), jnp.int32)
    for b in bounds[1:-1]:
        seg_id = seg_id + (jnp.arange(bounds[-1]) >= b).astype(jnp.int32)
    return (seg_id[:, None] == seg_id[None, :]).astype(BF16)


def _constants(seq):
    head_bounds = [0, HEAD_DIM, HEAD_DIM + ROPE_DIM, HEAD_PAD]
    pair_bounds = head_bounds + [HEAD_PAD + b for b in head_bounds[1:]]
    inv_head = jnp.concatenate([jnp.full((HEAD_DIM,), 1.0 / HEAD_DIM, F32),
                                jnp.full((HEAD_PAD - HEAD_DIM,), 1.0 / ROPE_DIM, F32)])
    half = ROPE_DIM // 2
    inv = 1.0 / (ROPE_THETA ** (jnp.arange(0, ROPE_DIM, 2, dtype=F32) / ROPE_DIM))
    ang = jnp.arange(seq, dtype=F32)[:, None] * inv[None, :]
    ones = jnp.ones((seq, HEAD_DIM), F32)
    zeros = jnp.zeros((seq, HEAD_DIM), F32)
    pad1 = jnp.ones((seq, HEAD_PAD - HEAD_DIM - ROPE_DIM), F32)
    cos = jnp.concatenate([ones, jnp.cos(ang), jnp.cos(ang), pad1], axis=-1)
    sin = jnp.concatenate([zeros, -jnp.sin(ang), jnp.sin(ang), 0.0 * pad1], axis=-1)
    assert cos.shape == (seq, HEAD_PAD) and half * 2 == ROPE_DIM
    return {
        "seg_mla": _segment_matrix(pair_bounds),
        "inv_mla": jnp.concatenate([inv_head, inv_head])[None, :],
        "seg64": _segment_matrix(list(range(0, PAIR + 1, HEAD_DIM))),
        "v_ones": jnp.broadcast_to(
            jnp.tile(jnp.concatenate([jnp.zeros((HEAD_DIM,), F32), jnp.ones((SUM_ROWS,), F32)]), N_MLA_HEADS)[:, None],
            (N_MLA_HEADS * V_AUG, LANES)),
        "cos": cos,
        "sin": sin,
    }


def _layer_weights(l, p):
    d = p["w_in"].shape[1]
    q_rank = p["g_q_lat"].shape[1]
    kv_rank = p["g_kv_lat"].shape[1]
    qk_dim = HEAD_DIM + ROPE_DIM
    pad = HEAD_PAD - qk_dim
    w_in = p["w_in"][l]
    o = q_rank + kv_rank
    w_kpe = w_in[:, o:o + ROPE_DIM]
    w_in_p = jnp.concatenate([
        w_in[:, :o],
        jnp.zeros((d, HEAD_DIM), F32), w_kpe, jnp.zeros((d, pad), F32),
        w_in[:, o + ROPE_DIM:],
    ], axis=1).astype(BF16)

    w_q = p["w_q_up"][l].reshape(q_rank, N_MLA_HEADS, qk_dim)
    w_q = jnp.pad(w_q, ((0, 0), (0, 0), (0, pad))).reshape(q_rank, N_MLA_HEADS * HEAD_PAD).astype(BF16)
    w_kv = p["w_kv_up"][l].reshape(kv_rank, N_MLA_HEADS, 2 * HEAD_DIM)
    w_k = jnp.pad(w_kv[:, :, :HEAD_DIM], ((0, 0), (0, 0), (0, HEAD_PAD - HEAD_DIM)))
    w_k = w_k.reshape(kv_rank, N_MLA_HEADS * HEAD_PAD).astype(BF16)
    w_v = jnp.pad(w_kv[:, :, HEAD_DIM:], ((0, 0), (0, 0), (0, SUM_ROWS)))
    w_v = w_v.reshape(kv_rank, N_MLA_HEADS * V_AUG).astype(BF16)

    gq = jnp.pad(p["g_q_mla"][l], (0, pad))
    gk = jnp.pad(p["g_k_mla"][l][:HEAD_DIM], (0, HEAD_PAD - HEAD_DIM))
    gkpe = jnp.pad(p["g_k_mla"][l][HEAD_DIM:], (HEAD_DIM, pad))
    g_out = p["g_out"][l]
    mla_w = N_MLA_HEADS * HEAD_DIM
    mem_w = N_MEM_HEADS * HEAD_DIM
    return {
        "g_mix": p["g_mix"][l][None, :],
        "w_in": w_in_p,
        "g_q_lat": p["g_q_lat"][l][None, :],
        "w_q": w_q,
        "g_kv_lat": p["g_kv_lat"][l][None, :],
        "w_k": w_k,
        "w_v_t": w_v.T,
        "gq_row": jnp.tile(gq * Q_SCALE, N_MLA_HEADS)[None, :],
        "gk_row": jnp.tile(gk, N_MLA_HEADS)[None, :],
        "gkpe_row": gkpe[None, :],
        "g_mem": p["g_mem"][l][None, :],
        "w_mem_kv": p["w_mem_kv"][l].astype(BF16),
        "gqm_row": jnp.tile(p["g_q_mem"][l], N_MEM_HEADS)[None, :],
        "gkm_row": jnp.tile(p["g_k_mem"][l], N_MEM_HEADS)[None, :],
        "conv_w": p["conv_w"][l],
        "g_out_mla": g_out[None, :mla_w],
        "g_out_mem": g_out[None, mla_w:mla_w + mem_w],
        "g_out_conv": g_out[None, mla_w + mem_w:],
        "w_out": p["w_out"][l].astype(BF16),
        "g_ffn": p["g_ffn"][l][None, :],
    }


def _tile_config(seq):
    return dict(tm=min(512, seq), tq=min(512, seq), tk=min(256, seq), tm_ffn=512, ffn_chunk=1408, moe_chunk=1408)


def _forward(p, *, tm, tq, tk, tm_ffn, ffn_chunk, moe_chunk):
    x = p["x"]
    batch, seq, d = x.shape
    depth = p["g_mix"].shape[0]
    assert seq % tm == 0 and seq % tq == 0 and (batch * seq) % tm_ffn == 0
    consts = _constants(seq)
    x2d = x.reshape(batch * seq, d)
    for l in range(depth):
        lw = _layer_weights(l, p)
        km, vm = _mem_kv(p["mem"], lw["g_mem"], lw["w_mem_kv"], consts["seg64"], lw["gkm_row"])
        q, k, vt, o_mem, o_conv = _proj(x2d, seq, tm, tk, lw, km, vm, consts["cos"], consts["sin"], consts)
        o_mla = _mla_attn(q, k, vt, batch, seq, tq)
        mix = (x2d, o_mla, o_mem, o_conv, lw["g_out_mla"], lw["w_out"])
        assert len(mix) == N_MIX_OPERANDS
        if l % 2 == 0:
            w_gu = p["w_dense_gu"][l // 2]
            ff = w_gu.shape[1] // 2
            x2d = _dense_ffn(mix, lw["g_ffn"], w_gu[:, :ff].astype(BF16), w_gu[:, ff:].astype(BF16),
                             p["w_dense_down"][l // 2].astype(BF16), tm_ffn, ffn_chunk)
        else:
            w_gu = p["w_expert_gu"][l // 2]
            ff = w_gu.shape[2] // 2
            w_router = jnp.pad(p["w_router"][l // 2], ((0, 0), (0, LANES - N_EXPERTS)))
            x2d = _out_proj(mix, tm)
            x2d = _moe_ffn(x2d, lw["g_ffn"], w_router, w_gu[:, :, :ff].astype(BF16), w_gu[:, :, ff:].astype(BF16),
                           p["w_expert_down"][l // 2].astype(BF16), tm_ffn, tm_ffn, moe_chunk)
    return x2d.reshape(batch, seq, d)


def kernel(x, mem, g_mix, w_in, g_q_lat, w_q_up, g_kv_lat, w_kv_up, g_q_mla, g_k_mla, g_mem, w_mem_kv, g_q_mem,
           g_k_mem, conv_w, g_out, w_out, g_ffn, w_dense_gu, w_dense_down, w_router, w_expert_gu, w_expert_down):
    p = dict(x=x, mem=mem, g_mix=g_mix, w_in=w_in, g_q_lat=g_q_lat, w_q_up=w_q_up, g_kv_lat=g_kv_lat,
             w_kv_up=w_kv_up, g_q_mla=g_q_mla, g_k_mla=g_k_mla, g_mem=g_mem, w_mem_kv=w_mem_kv, g_q_mem=g_q_mem,
             g_k_mem=g_k_mem, conv_w=conv_w, g_out=g_out, w_out=w_out, g_ffn=g_ffn, w_dense_gu=w_dense_gu,
             w_dense_down=w_dense_down, w_router=w_router, w_expert_gu=w_expert_gu, w_expert_down=w_expert_down)
    return _forward(p, **_tile_config(x.shape[1]))
```

```python
import functools
import math

import jax
import jax.numpy as jnp
from jax import lax
from jax.experimental import pallas as pl
from jax.experimental.pallas import tpu as pltpu

F32 = jnp.float32
BF16 = jnp.bfloat16

EPS = 1e-6
ROPE_THETA = 10000.0
HEAD_DIM = 64
ROPE_DIM = 32
N_MLA_HEADS = 8
N_MEM_HEADS = 4
CONV_K = 3
N_EXPERTS = 8
LANES = 128
HEAD_PAD = LANES
PAIR = 2 * LANES
ATTN_HEADS = 8
CARRY_ROWS = 8
SUM_ROWS = 16
V_AUG = HEAD_DIM + SUM_ROWS
VMEM_LIMIT = 56 * 1024 * 1024

NT_DIMS = (((1,), (1,)), ((), ()))
Q_SCALE = math.log2(math.e) / math.sqrt(HEAD_DIM + ROPE_DIM)


def _rms(x, g):
    return x * lax.rsqrt(jnp.mean(x * x, axis=-1, keepdims=True) + EPS) * g


def _seg_sumsq(x, seg):
    w = seg.shape[0]
    x2 = (x * x).astype(BF16)
    parts = [jnp.dot(x2[:, c:c + w], seg, preferred_element_type=F32) for c in range(0, x.shape[1], w)]
    return parts[0] if len(parts) == 1 else jnp.concatenate(parts, axis=-1)


def _rope_head(y, cos, sin_signed, lane):
    up = pltpu.roll(y, LANES - ROPE_DIM // 2, 1)
    down = pltpu.roll(y, ROPE_DIM // 2, 1)
    swapped = jnp.where(lane < HEAD_DIM + ROPE_DIM // 2, up, down)
    return y * cos + swapped * sin_signed


def _mem_kv_kernel(mem_ref, g_mem_ref, w_ref, seg_ref, gk_ref, km_ref, vm_ref):
    width = km_ref.shape[-1]
    mn = _rms(mem_ref[...], g_mem_ref[...]).astype(BF16)
    kv = jnp.dot(mn, w_ref[...], preferred_element_type=F32)
    km = kv[:, :width]
    ss = _seg_sumsq(km, seg_ref[...])
    km_ref[...] = (km * lax.rsqrt(ss * (1.0 / HEAD_DIM) + EPS) * gk_ref[...]).astype(BF16)
    vm_ref[...] = kv[:, width:].astype(BF16)


def _mem_kv(mem, g_mem, w_mem_kv, seg64, gk_row):
    b, m, d = mem.shape
    width = w_mem_kv.shape[1] // 2
    const = lambda i: (0, 0)
    return pl.pallas_call(
        _mem_kv_kernel,
        grid=(b,),
        in_specs=[
            pl.BlockSpec((None, m, d), lambda i: (i, 0, 0)),
            pl.BlockSpec((1, d), const),
            pl.BlockSpec((d, 2 * width), const),
            pl.BlockSpec((PAIR, PAIR), const),
            pl.BlockSpec((1, width), const),
        ],
        out_specs=[
            pl.BlockSpec((None, m, width), lambda i: (i, 0, 0)),
            pl.BlockSpec((None, m, width), lambda i: (i, 0, 0)),
        ],
        out_shape=[jax.ShapeDtypeStruct((b, m, width), BF16)] * 2,
        compiler_params=pltpu.CompilerParams(dimension_semantics=("parallel",), vmem_limit_bytes=VMEM_LIMIT),
        name="mem_kv",
    )(mem, g_mem, w_mem_kv, seg64, gk_row)


def _proj_kernel(x_ref, g_mix_ref, w_in_ref, g_qlat_ref, w_q_ref, g_kvlat_ref, w_k_ref, w_v_ref, v_ones_ref,
                 seg_mla_ref, inv_mla_ref, gq_ref, gk_ref, gkpe_ref, cos_ref, sin_ref,
                 km_ref, vm_ref, seg64_ref, gqm_ref, conv_w_ref, g_omem_ref, g_oconv_ref,
                 q_ref, k_ref, v_ref, omem_ref, oconv_ref, cu_ref, *, tiles_per_seq):
    tm = x_ref.shape[0]
    q_rank = g_qlat_ref.shape[1]
    kv_rank = g_kvlat_ref.shape[1]
    mem_w = gqm_ref.shape[1]
    conv_c = conv_w_ref.shape[1]

    h = _rms(x_ref[...], g_mix_ref[...]).astype(BF16)
    z = jnp.dot(h, w_in_ref[...], preferred_element_type=F32)
    o = 0
    q_lat = z[:, o:o + q_rank]; o += q_rank
    kv_lat = z[:, o:o + kv_rank]; o += kv_rank
    kpe = z[:, o:o + HEAD_PAD]; o += HEAD_PAD
    q_mem = z[:, o:o + mem_w]; o += mem_w
    gate_b = z[:, o:o + conv_c]; o += conv_c
    gate_c = z[:, o:o + conv_c]; o += conv_c
    u = z[:, o:o + conv_c]

    lane = lax.broadcasted_iota(jnp.int32, (tm, LANES), 1)
    cos = cos_ref[...]
    sin = sin_ref[...]
    seg_mla = seg_mla_ref[...]
    inv_mla = inv_mla_ref[...]

    qn = _rms(q_lat, g_qlat_ref[...]).astype(BF16)
    qf = jnp.dot(qn, w_q_ref[...], preferred_element_type=F32)
    n_rep = qf.shape[1] // PAIR
    inv_row = jnp.concatenate([inv_mla] * n_rep, axis=-1)
    qy = qf * lax.rsqrt(_seg_sumsq(qf, seg_mla) * inv_row + EPS) * gq_ref[...]
    for hd in range(qf.shape[1] // HEAD_PAD):
        sl = slice(hd * HEAD_PAD, (hd + 1) * HEAD_PAD)
        q_ref[:, sl] = _rope_head(qy[:, sl], cos, sin, lane).astype(BF16)

    kvn = _rms(kv_lat, g_kvlat_ref[...]).astype(BF16)
    kf = jnp.dot(kvn, w_k_ref[...], preferred_element_type=F32)
    vt = lax.dot_general(w_v_ref[...], kvn, NT_DIMS, preferred_element_type=F32)
    tk = v_ref.shape[-1]
    ones_rows = jnp.concatenate([v_ones_ref[...]] * (tk // LANES), axis=1)
    for c in range(tm // tk):
        v_ref[c] = (vt[:, c * tk:(c + 1) * tk] + ones_rows).astype(BF16)
    ky = kf * lax.rsqrt(_seg_sumsq(kf, seg_mla) * inv_row + EPS) * gk_ref[...]
    kpe_y = kpe * lax.rsqrt(_seg_sumsq(kpe, seg_mla[:LANES, :LANES]) * inv_mla[:, :LANES] + EPS) * gkpe_ref[...]
    kpe_r = _rope_head(kpe_y, cos, sin, lane)
    for hd in range(kf.shape[1] // HEAD_PAD):
        sl = slice(hd * HEAD_PAD, (hd + 1) * HEAD_PAD)
        k_ref[:, sl] = (ky[:, sl] + kpe_r).astype(BF16)

    qm = q_mem * lax.rsqrt(_seg_sumsq(q_mem, seg64_ref[...]) * (1.0 / HEAD_DIM) + EPS) * gqm_ref[...]
    km = km_ref[...]
    vm = vm_ref[...]
    lane_m = lax.broadcasted_iota(jnp.int32, (km.shape[0], LANES), 1)
    groups = []
    for grp in range(mem_w // LANES):
        sl = slice(grp * LANES, (grp + 1) * LANES)
        qg, kg, vg = qm[:, sl], km[:, sl], vm[:, sl]
        og = None
        for half in range(LANES // HEAD_DIM):
            in_head = (lane >= half * HEAD_DIM) & (lane < (half + 1) * HEAD_DIM)
            in_head_m = (lane_m >= half * HEAD_DIM) & (lane_m < (half + 1) * HEAD_DIM)
            qh = jnp.where(in_head, qg, 0.0).astype(BF16)
            s = lax.dot_general(qh, kg, NT_DIMS, preferred_element_type=F32) * (1.0 / math.sqrt(HEAD_DIM))
            p = jnp.exp(s - jnp.max(s, axis=-1, keepdims=True))
            p = (p * (1.0 / jnp.sum(p, axis=-1, keepdims=True))).astype(BF16)
            vh = jnp.where(in_head_m, vg, jnp.zeros_like(vg))
            oh = jnp.dot(p, vh, preferred_element_type=F32)
            og = oh if og is None else og + oh
        groups.append(og)
    o_mem = jnp.concatenate(groups, axis=-1)
    omem_ref[...] = _rms(o_mem, g_omem_ref[...]).astype(BF16)

    @pl.when(pl.program_id(0) % tiles_per_seq == 0)
    def _():
        cu_ref[0:CARRY_ROWS, :] = jnp.zeros((CARRY_ROWS, conv_c), F32)

    cu = gate_c * u
    cu_ref[CARRY_ROWS:CARRY_ROWS + tm, :] = cu
    w = conv_w_ref[...]
    conv = w[CONV_K - 1:CONV_K] * cu
    for tap in range(CONV_K - 1):
        shift = CONV_K - 1 - tap
        conv = conv + w[tap:tap + 1] * cu_ref[CARRY_ROWS - shift:CARRY_ROWS - shift + tm, :]
    oconv_ref[...] = _rms(gate_b * conv, g_oconv_ref[...]).astype(BF16)
    cu_ref[0:CARRY_ROWS, :] = cu_ref[tm:tm + CARRY_ROWS, :]


def _proj(x2d, seq, tm, tk, lw, km, vm, cos, sin, consts):
    t, d = x2d.shape
    tiles_per_seq = seq // tm
    n_q = lw["w_q"].shape[1]
    n_v = lw["w_v_t"].shape[0]
    mem_w = km.shape[-1]
    conv_c = lw["conv_w"].shape[1]
    const = lambda i: (0, 0)
    row = lambda i: (i, 0)
    full = lambda a: pl.BlockSpec(a.shape, const)
    operands = [
        (x2d, pl.BlockSpec((tm, d), row)),
        (lw["g_mix"], None), (lw["w_in"], None), (lw["g_q_lat"], None), (lw["w_q"], None),
        (lw["g_kv_lat"], None), (lw["w_k"], None), (lw["w_v_t"], None), (consts["v_ones"], None),
        (consts["seg_mla"], None), (consts["inv_mla"], None),
        (lw["gq_row"], None), (lw["gk_row"], None), (lw["gkpe_row"], None),
        (cos, pl.BlockSpec((tm, LANES), lambda i: (i % tiles_per_seq, 0))),
        (sin, pl.BlockSpec((tm, LANES), lambda i: (i % tiles_per_seq, 0))),
        (km, pl.BlockSpec((None,) + km.shape[1:], lambda i: (i // tiles_per_seq, 0, 0))),
        (vm, pl.BlockSpec((None,) + vm.shape[1:], lambda i: (i // tiles_per_seq, 0, 0))),
        (consts["seg64"], None), (lw["gqm_row"], None), (lw["conv_w"], None),
        (lw["g_out_mem"], None), (lw["g_out_conv"], None),
    ]
    args = [a for a, _ in operands]
    specs = [full(a) if s is None else s for a, s in operands]
    out_widths = (n_q, n_q, mem_w, conv_c)
    row_out = [(pl.BlockSpec((tm, w), row), jax.ShapeDtypeStruct((t, w), BF16)) for w in out_widths]
    vt_out = (pl.BlockSpec((tm // tk, n_v, tk), lambda i: (i, 0, 0)), jax.ShapeDtypeStruct((t // tk, n_v, tk), BF16))
    outs = row_out[:2] + [vt_out] + row_out[2:]
    return pl.pallas_call(
        functools.partial(_proj_kernel, tiles_per_seq=tiles_per_seq),
        grid=(t // tm,),
        in_specs=specs,
        out_specs=[spec for spec, _ in outs],
        out_shape=[shape for _, shape in outs],
        scratch_shapes=[pltpu.VMEM((tm + CARRY_ROWS, conv_c), F32)],
        compiler_params=pltpu.CompilerParams(dimension_semantics=("arbitrary",), vmem_limit_bytes=VMEM_LIMIT),
        name="proj",
    )(*args)


def _mla_attn_kernel(q_ref, k_ref, vt_ref, o_ref, s_scr):
    tq = q_ref.shape[0]
    tk = vt_ref.shape[-1]
    assert tq == 2 * tk and s_scr.shape[0] == 2
    qi = pl.program_id(2)
    neg = jnp.finfo(F32).min
    visible = lax.broadcasted_iota(jnp.int32, (tk, tk), 0) <= lax.broadcasted_iota(jnp.int32, (tk, tk), 1)

    n_heads = q_ref.shape[1] // HEAD_PAD
    head_lanes = [slice(hh * HEAD_PAD, (hh + 1) * HEAD_PAD) for hh in range(n_heads)]
    head_rows = [slice(hh * V_AUG, (hh + 1) * V_AUG) for hh in range(n_heads)]
    qs = [q_ref[:, sl] for sl in head_lanes]

    def key_block(j, hh):
        return k_ref[pl.ds(pl.multiple_of(j * tk, tk), tk), head_lanes[hh]]

    def scores_into(slot, j):
        for hh in range(n_heads):
            s_scr[slot, hh] = lax.dot_general(key_block(j, hh), qs[hh], NT_DIMS, preferred_element_type=F32)

    def update(state, s, vt_h):
        m, acc = state
        m_new = jnp.maximum(m, jnp.max(s, axis=0, keepdims=True))
        p = jnp.exp2(s - m_new).astype(BF16)
        return m_new, jnp.exp2(m - m_new) * acc + jnp.dot(vt_h, p, preferred_element_type=F32)

    def consume(slot, j, carry):
        vt = vt_ref[j]
        return tuple(update(carry[hh], s_scr[slot, hh], vt[head_rows[hh], :]) for hh in range(n_heads))

    def pair(i, carry):
        scores_into(1, 2 * i + 1)
        carry = consume(0, 2 * i, carry)
        scores_into(0, 2 * i + 2)
        return consume(1, 2 * i + 1, carry)

    init = tuple((jnp.full((1, tq), -jnp.inf, F32), jnp.zeros((V_AUG, tq), F32)) for _ in range(n_heads))
    scores_into(0, 0)
    carry = lax.fori_loop(0, qi, pair, init)

    vt_a = vt_ref[2 * qi]
    vt_b = vt_ref[2 * qi + 1]
    outs = []
    for hh in range(n_heads):
        m, acc = carry[hh]
        s_a = s_scr[0, hh]
        s_b = lax.dot_general(key_block(2 * qi + 1, hh), qs[hh][tk:, :], NT_DIMS, preferred_element_type=F32)
        va, vb = vt_a[head_rows[hh], :], vt_b[head_rows[hh], :]
        _, acc_l = update((m[:, :tk], acc[:, :tk]), jnp.where(visible, s_a[:, :tk], neg), va)
        right = update((m[:, tk:], acc[:, tk:]), s_a[:, tk:], va)
        _, acc_r = update(right, jnp.where(visible, s_b, neg), vb)
        acc = jnp.concatenate([acc_l, acc_r], axis=1)
        outs.append(acc[:HEAD_DIM] * (1.0 / acc[HEAD_DIM:HEAD_DIM + 1]))
    o_ref[...] = jnp.concatenate(outs, axis=0).T.astype(BF16)


def _mla_attn(q, k, vt, batch, seq, tq):
    t = q.shape[0]
    group = ATTN_HEADS * HEAD_PAD
    n_pairs = q.shape[1] // group
    nq = seq // tq
    _, v_rows, tk = vt.shape
    nk = seq // tk
    return pl.pallas_call(
        _mla_attn_kernel,
        grid=(batch, n_pairs, nq),
        in_specs=[
            pl.BlockSpec((tq, group), lambda b, hp, i: (b * nq + i, hp)),
            pl.BlockSpec((seq, group), lambda b, hp, i: (b, hp)),
            pl.BlockSpec((nk, v_rows // n_pairs, tk), lambda b, hp, i: (b, hp, 0)),
        ],
        out_specs=pl.BlockSpec((tq, ATTN_HEADS * HEAD_DIM), lambda b, hp, i: (b * nq + i, hp)),
        out_shape=jax.ShapeDtypeStruct((t, n_pairs * ATTN_HEADS * HEAD_DIM), BF16),
        scratch_shapes=[pltpu.VMEM((2, ATTN_HEADS, tk, tq), F32)],
        compiler_params=pltpu.CompilerParams(dimension_semantics=("parallel", "parallel", "parallel"),
                                             vmem_limit_bytes=VMEM_LIMIT),
        name="mla_attn",
    )(q, k, vt)


N_MIX_OPERANDS = 6


def _mixer_residual(x_ref, omla_ref, omem_ref, oconv_ref, g_ref, w_ref):
    a = _rms(omla_ref[...].astype(F32), g_ref[...]).astype(BF16)
    cat = jnp.concatenate([a, omem_ref[...], oconv_ref[...]], axis=-1)
    return x_ref[...] + jnp.dot(cat, w_ref[...], preferred_element_type=F32)


def _mixer_specs(mix, tm):
    row = lambda i: (i, 0)
    const = lambda i: (0, 0)
    return ([pl.BlockSpec((tm, a.shape[1]), row) for a in mix[:4]]
            + [pl.BlockSpec(mix[4].shape, const), pl.BlockSpec(mix[5].shape, const, pipeline_mode=pl.Buffered(1))])


def _out_proj_kernel(*refs):
    refs[N_MIX_OPERANDS][...] = _mixer_residual(*refs[:N_MIX_OPERANDS])


def _out_proj(mix, tm):
    t, d = mix[0].shape
    return pl.pallas_call(
        _out_proj_kernel,
        grid=(t // tm,),
        in_specs=_mixer_specs(mix, tm),
        out_specs=pl.BlockSpec((tm, d), lambda i: (i, 0)),
        out_shape=jax.ShapeDtypeStruct((t, d), F32),
        compiler_params=pltpu.CompilerParams(dimension_semantics=("parallel",), vmem_limit_bytes=VMEM_LIMIT),
        name="out_proj",
    )(*mix)


def _swiglu_part(h, wgu_ref, wd_ref, chunks):
    ff = wd_ref.shape[0]
    out = None
    for a, b in chunks:
        if (a, b) == (0, ff):
            gu = jnp.dot(h, wgu_ref[...], preferred_element_type=F32)
            g, u = gu[:, :ff], gu[:, ff:]
        else:
            g = jnp.dot(h, wgu_ref[:, a:b], preferred_element_type=F32)
            u = jnp.dot(h, wgu_ref[:, ff + a:ff + b], preferred_element_type=F32)
        act = (g * (1.0 / (1.0 + jnp.exp(-g))) * u).astype(BF16)
        part = jnp.dot(act, wd_ref[a:b, :], preferred_element_type=F32)
        out = part if out is None else out + part
    return out


def _dense_ffn_kernel(*refs, chunks):
    mix_refs, (g_ref, wgu_ref, wd_ref, y_ref) = refs[:N_MIX_OPERANDS], refs[N_MIX_OPERANDS:]
    x = _mixer_residual(*mix_refs)
    h = _rms(x, g_ref[...]).astype(BF16)
    y_ref[...] = x + _swiglu_part(h, wgu_ref, wd_ref, chunks)


def _dense_ffn(mix, g_ffn, wgu, wd, tm, chunks):
    t, d = mix[0].shape
    row = lambda i: (i, 0)
    const = lambda i: (0, 0)
    resident = lambda a: pl.BlockSpec(a.shape, const, pipeline_mode=pl.Buffered(1))
    return pl.pallas_call(
        functools.partial(_dense_ffn_kernel, chunks=chunks),
        grid=(t // tm,),
        in_specs=_mixer_specs(mix, tm) + [pl.BlockSpec(g_ffn.shape, const), resident(wgu), resident(wd)],
        out_specs=pl.BlockSpec((tm, d), row),
        out_shape=jax.ShapeDtypeStruct((t, d), F32),
        compiler_params=pltpu.CompilerParams(dimension_semantics=("parallel",), vmem_limit_bytes=VMEM_LIMIT),
        name="dense_ffn",
    )(*mix, g_ffn, wgu, wd)


ROUTE_ROWS = 8
TOP_K = 2
DMA_UNROLL = 8
ROW_TILE = 8


def _moe_route_kernel(x_ref, g_ref, wr_ref, tri_ref, rw_ref, ri_ref, cnt_ref, base_scr):
    tm = x_ref.shape[0]
    lane = lax.broadcasted_iota(jnp.int32, (tm, LANES), 1)
    lane_f = lane.astype(F32)

    @pl.when(pl.program_id(0) == 0)
    def _():
        base_scr[...] = jnp.zeros_like(base_scr)

    hf = _rms(x_ref[...], g_ref[...])
    wr = wr_ref[...]
    h_hi = hf.astype(BF16)
    h_lo = (hf - h_hi.astype(F32)).astype(BF16)
    w_hi = wr.astype(BF16)
    w_lo = (wr - w_hi.astype(F32)).astype(BF16)
    logits = (jnp.dot(h_hi, w_hi, preferred_element_type=F32) + jnp.dot(h_hi, w_lo, preferred_element_type=F32)
              + jnp.dot(h_lo, w_hi, preferred_element_type=F32))
    logits = jnp.where(lane < N_EXPERTS, logits, -jnp.inf)
    m1 = jnp.max(logits, axis=-1, keepdims=True)
    i1 = jnp.min(jnp.where(logits == m1, lane_f, float(LANES)), axis=-1, keepdims=True)
    rest = jnp.where(lane_f == i1, -jnp.inf, logits)
    m2 = jnp.max(rest, axis=-1, keepdims=True)
    i2 = jnp.min(jnp.where(rest == m2, lane_f, float(LANES)), axis=-1, keepdims=True)
    e2 = jnp.exp(m2 - m1)
    w1 = 1.0 / (1.0 + e2)
    rw_ref[...] = jnp.where(lane == 0, w1, jnp.where(lane == 1, e2 * w1, 0.0))

    chosen = jnp.where((lane_f == i1) | (lane_f == i2), 1.0, 0.0)
    before = jnp.dot(tri_ref[...], chosen.astype(BF16), preferred_element_type=F32) + base_scr[...]
    r1 = jnp.sum(jnp.where(lane_f == i1, before, 0.0), axis=-1, keepdims=True)
    r2 = jnp.sum(jnp.where(lane_f == i2, before, 0.0), axis=-1, keepdims=True)
    base_scr[...] += jnp.sum(chosen, axis=0, keepdims=True)
    cnt_ref[...] = base_scr[...].astype(jnp.int32)
    table = jnp.where(lane == 0, i1, jnp.where(lane == 1, i2, jnp.where(lane == 2, r1, jnp.where(lane == 3, r2, 0.0))))
    ri_ref[...] = table.T[:ROUTE_ROWS, :].astype(jnp.int32)


def _moe_route(x2d, g_ffn, w_router, tm):
    t, d = x2d.shape
    tri = (jnp.arange(tm)[:, None] > jnp.arange(tm)[None, :]).astype(BF16)
    const = lambda i: (0, 0)
    return pl.pallas_call(
        _moe_route_kernel,
        grid=(t // tm,),
        in_specs=[pl.BlockSpec((tm, d), lambda i: (i, 0)), pl.BlockSpec(g_ffn.shape, const),
                  pl.BlockSpec(w_router.shape, const), pl.BlockSpec((tm, tm), const)],
        out_specs=[pl.BlockSpec((tm, LANES), lambda i: (i, 0)),
                   pl.BlockSpec((None, ROUTE_ROWS, tm), lambda i: (i, 0, 0)),
                   pl.BlockSpec((1, LANES), const)],
        out_shape=[jax.ShapeDtypeStruct((t, LANES), F32),
                   jax.ShapeDtypeStruct((t // tm, ROUTE_ROWS, tm), jnp.int32),
                   jax.ShapeDtypeStruct((1, LANES), jnp.int32)],
        scratch_shapes=[pltpu.VMEM((1, LANES), F32)],
        compiler_params=pltpu.CompilerParams(dimension_semantics=("arbitrary",), vmem_limit_bytes=VMEM_LIMIT),
        name="moe_route",
    )(x2d, g_ffn, w_router, tri)


def _token_copy(src_ref, src_tok, dst_ref, dst_tok, sem):
    src = src_ref.at[pl.ds(pl.multiple_of(src_tok * ROW_TILE, ROW_TILE), ROW_TILE)]
    dst = dst_ref.at[pl.ds(pl.multiple_of(dst_tok * ROW_TILE, ROW_TILE), ROW_TILE)]
    return pltpu.make_async_copy(src, dst, sem)


def _to_token_tiles(ref, value):
    rows = value.shape[0]
    for c in range(ROW_TILE):
        ref[pl.ds(c, rows, stride=ROW_TILE), :] = value[:, c * LANES:(c + 1) * LANES]


def _from_token_tiles(ref, rows):
    return jnp.concatenate([ref[pl.ds(c, rows, stride=ROW_TILE), :] for c in range(ROW_TILE)], axis=-1)


def _moe_dispatch_kernel(zlo_ref, zhi_ref, x_ref, g_ref, pos_ref, xs_ref, h_scr, zero_scr, sem):
    tm = x_ref.shape[0]
    _to_token_tiles(h_scr, _rms(x_ref[...], g_ref[...]))

    def start(r, c):
        for k in range(TOP_K):
            _token_copy(h_scr, r, xs_ref, pos_ref[0, k * tm + r], sem).start(priority=k)
        return c

    def wait(r, c):
        for k in range(TOP_K):
            _token_copy(h_scr, 0, xs_ref, 0, sem).wait()
        return c

    lax.fori_loop(0, tm, start, 0, unroll=DMA_UNROLL)
    lax.fori_loop(0, tm, wait, 0, unroll=DMA_UNROLL)

    @pl.when(pl.program_id(0) == pl.num_programs(0) - 1)
    def _():
        zero_scr[...] = jnp.zeros_like(zero_scr)
        for e in range(N_EXPERTS):
            def zstart(r, c):
                _token_copy(zero_scr, 0, xs_ref, r, sem).start()
                return c

            def zwait(r, c):
                _token_copy(zero_scr, 0, xs_ref, 0, sem).wait()
                return c

            lax.fori_loop(zlo_ref[e], zhi_ref[e], zstart, 0)
            lax.fori_loop(zlo_ref[e], zhi_ref[e], zwait, 0)


def _moe_dispatch(x2d, g_ffn, pos, zero_lo, zero_hi, n_rows, tm):
    t, d = x2d.shape
    assert d == ROW_TILE * LANES
    return pl.pallas_call(
        _moe_dispatch_kernel,
        grid_spec=pltpu.PrefetchScalarGridSpec(
            num_scalar_prefetch=2,
            grid=(t // tm,),
            in_specs=[pl.BlockSpec((tm, d), lambda i, *_: (i, 0)),
                      pl.BlockSpec(g_ffn.shape, lambda i, *_: (0, 0)),
                      pl.BlockSpec((None, 1, TOP_K * tm), lambda i, *_: (i, 0, 0), memory_space=pltpu.SMEM)],
            out_specs=pl.BlockSpec(memory_space=pl.ANY),
            scratch_shapes=[pltpu.VMEM((tm * ROW_TILE, LANES), F32), pltpu.VMEM((ROW_TILE, LANES), F32),
                            pltpu.SemaphoreType.DMA(())],
        ),
        out_shape=jax.ShapeDtypeStruct((n_rows * ROW_TILE, LANES), F32),
        compiler_params=pltpu.CompilerParams(dimension_semantics=("arbitrary",), vmem_limit_bytes=VMEM_LIMIT),
        name="moe_dispatch",
    )(zero_lo, zero_hi, x2d, g_ffn, pos)


def _moe_gmm_kernel(te_ref, xs_ref, wgu_ref, wd_ref, ys_ref, *, chunks):
    del te_ref
    tg = xs_ref.shape[0] // ROW_TILE
    x = _from_token_tiles(xs_ref, tg).astype(BF16)
    _to_token_tiles(ys_ref, _swiglu_part(x, wgu_ref, wd_ref, chunks))


def _moe_gmm(xs, tile_expert, wgu, wd, tg, chunks):
    _, ff, d = wd.shape
    n_tiles = xs.shape[0] // (tg * ROW_TILE)
    row = lambda i, te: (i, 0)
    by_expert = lambda i, te: (te[i], 0, 0)
    return pl.pallas_call(
        functools.partial(_moe_gmm_kernel, chunks=chunks),
        grid_spec=pltpu.PrefetchScalarGridSpec(
            num_scalar_prefetch=1,
            grid=(n_tiles,),
            in_specs=[pl.BlockSpec((tg * ROW_TILE, LANES), row),
                      pl.BlockSpec((None, d, 2 * ff), by_expert),
                      pl.BlockSpec((None, ff, d), by_expert)],
            out_specs=pl.BlockSpec((tg * ROW_TILE, LANES), row),
        ),
        out_shape=jax.ShapeDtypeStruct(xs.shape, F32),
        compiler_params=pltpu.CompilerParams(dimension_semantics=("arbitrary",), vmem_limit_bytes=VMEM_LIMIT),
        name="moe_gmm",
    )(tile_expert, xs, wgu, wd)


def _moe_combine_kernel(x_ref, rw_ref, pos_ref, ys_ref, y_ref, a_scr, b_scr, sem):
    tm = x_ref.shape[0]
    bufs = (a_scr, b_scr)

    def start(r, c):
        for k in range(TOP_K):
            _token_copy(ys_ref, pos_ref[0, k * tm + r], bufs[k], r, sem).start(priority=k)
        return c

    def wait(r, c):
        for k in range(TOP_K):
            _token_copy(ys_ref, 0, bufs[k], 0, sem).wait()
        return c

    lax.fori_loop(0, tm, start, 0, unroll=DMA_UNROLL)
    lax.fori_loop(0, tm, wait, 0, unroll=DMA_UNROLL)
    rw = rw_ref[...]
    y_ref[...] = x_ref[...] + rw[:, 0:1] * _from_token_tiles(a_scr, tm) + rw[:, 1:2] * _from_token_tiles(b_scr, tm)


def _moe_combine(x2d, route_w, pos, ys, tm):
    t, d = x2d.shape
    return pl.pallas_call(
        _moe_combine_kernel,
        grid=(t // tm,),
        in_specs=[pl.BlockSpec((tm, d), lambda i: (i, 0)),
                  pl.BlockSpec((tm, LANES), lambda i: (i, 0)),
                  pl.BlockSpec((None, 1, TOP_K * tm), lambda i: (i, 0, 0), memory_space=pltpu.SMEM),
                  pl.BlockSpec(memory_space=pl.ANY)],
        out_specs=pl.BlockSpec((tm, d), lambda i: (i, 0)),
        out_shape=jax.ShapeDtypeStruct((t, d), F32),
        scratch_shapes=[pltpu.VMEM((tm * ROW_TILE, LANES), F32), pltpu.VMEM((tm * ROW_TILE, LANES), F32),
                        pltpu.SemaphoreType.DMA(())],
        compiler_params=pltpu.CompilerParams(dimension_semantics=("arbitrary",), vmem_limit_bytes=VMEM_LIMIT),
        name="moe_combine",
    )(x2d, route_w, pos, ys)


def _moe_ffn(x2d, g_ffn, w_router, wgu, wd, tm, tg, chunks):
    t, _ = x2d.shape
    n_e = wd.shape[0]
    route_w, route_i, counts = _moe_route(x2d, g_ffn, w_router, tm)
    cnt = counts[0, :n_e]
    tiles = (cnt + tg - 1) // tg
    tile_end = jnp.cumsum(tiles)
    off = ((tile_end - tiles) * tg).astype(jnp.int32)
    n_tiles = (TOP_K * t) // tg + n_e
    tile_ids = jnp.arange(n_tiles, dtype=jnp.int32)
    tile_expert = jnp.sum((tile_ids[:, None] >= tile_end[None, :]).astype(jnp.int32), axis=1)
    tile_expert = jnp.minimum(tile_expert, n_e - 1).astype(jnp.int32)
    zero_hi = (tile_end * tg).astype(jnp.int32).at[n_e - 1].set(n_tiles * tg)
    pos = jnp.take(off, route_i[:, :TOP_K, :]) + route_i[:, TOP_K:2 * TOP_K, :]
    pos = pos.reshape(t // tm, 1, TOP_K * tm)
    xs = _moe_dispatch(x2d, g_ffn, pos, off + cnt, zero_hi, n_tiles * tg, tm)
    ys = _moe_gmm(xs, tile_expert, wgu, wd, tg, chunks)
    return _moe_combine(x2d, route_w, pos, ys, tm)


def _segment_matrix(bounds):
    seg_id = jnp.zeros((bounds[-1],), jnp.int32)
    for b in bounds[1:-1]:
        seg_id = seg_id + (jnp.arange(bounds[-1]) >= b).astype(jnp.int32)
    return (seg_id[:, None] == seg_id[None, :]).astype(BF16)


def _constants(seq):
    head_bounds = [0, HEAD_DIM, HEAD_DIM + ROPE_DIM, HEAD_PAD]
    pair_bounds = head_bounds + [HEAD_PAD + b for b in head_bounds[1:]]
    inv_head = jnp.concatenate([jnp.full((HEAD_DIM,), 1.0 / HEAD_DIM, F32),
                                jnp.full((HEAD_PAD - HEAD_DIM,), 1.0 / ROPE_DIM, F32)])
    half = ROPE_DIM // 2
    inv = 1.0 / (ROPE_THETA ** (jnp.arange(0, ROPE_DIM, 2, dtype=F32) / ROPE_DIM))
    ang = jnp.arange(seq, dtype=F32)[:, None] * inv[None, :]
    ones = jnp.ones((seq, HEAD_DIM), F32)
    zeros = jnp.zeros((seq, HEAD_DIM), F32)
    pad1 = jnp.ones((seq, HEAD_PAD - HEAD_DIM - ROPE_DIM), F32)
    cos = jnp.concatenate([ones, jnp.cos(ang), jnp.cos(ang), pad1], axis=-1)
    sin = jnp.concatenate([zeros, -jnp.sin(ang), jnp.sin(ang), 0.0 * pad1], axis=-1)
    assert cos.shape == (seq, HEAD_PAD) and half * 2 == ROPE_DIM
    return {
        "seg_mla": _segment_matrix(pair_bounds),
        "inv_mla": jnp.concatenate([inv_head, inv_head])[None, :],
        "seg64": _segment_matrix(list(range(0, PAIR + 1, HEAD_DIM))),
        "v_ones": jnp.broadcast_to(
            jnp.tile(jnp.concatenate([jnp.zeros((HEAD_DIM,), F32), jnp.ones((SUM_ROWS,), F32)]), N_MLA_HEADS)[:, None],
            (N_MLA_HEADS * V_AUG, LANES)),
        "cos": cos,
        "sin": sin,
    }


def _layer_weights(l, p):
    d = p["w_in"].shape[1]
    q_rank = p["g_q_lat"].shape[1]
    kv_rank = p["g_kv_lat"].shape[1]
    qk_dim = HEAD_DIM + ROPE_DIM
    pad = HEAD_PAD - qk_dim
    w_in = p["w_in"][l]
    o = q_rank + kv_rank
    w_kpe = w_in[:, o:o + ROPE_DIM]
    w_in_p = jnp.concatenate([
        w_in[:, :o],
        jnp.zeros((d, HEAD_DIM), F32), w_kpe, jnp.zeros((d, pad), F32),
        w_in[:, o + ROPE_DIM:],
    ], axis=1).astype(BF16)

    w_q = p["w_q_up"][l].reshape(q_rank, N_MLA_HEADS, qk_dim)
    w_q = jnp.pad(w_q, ((0, 0), (0, 0), (0, pad))).reshape(q_rank, N_MLA_HEADS * HEAD_PAD).astype(BF16)
    w_kv = p["w_kv_up"][l].reshape(kv_rank, N_MLA_HEADS, 2 * HEAD_DIM)
    w_k = jnp.pad(w_kv[:, :, :HEAD_DIM], ((0, 0), (0, 0), (0, HEAD_PAD - HEAD_DIM)))
    w_k = w_k.reshape(kv_rank, N_MLA_HEADS * HEAD_PAD).astype(BF16)
    w_v = jnp.pad(w_kv[:, :, HEAD_DIM:], ((0, 0), (0, 0), (0, SUM_ROWS)))
    w_v = w_v.reshape(kv_rank, N_MLA_HEADS * V_AUG).astype(BF16)

    gq = jnp.pad(p["g_q_mla"][l], (0, pad))
    gk = jnp.pad(p["g_k_mla"][l][:HEAD_DIM], (0, HEAD_PAD - HEAD_DIM))
    gkpe = jnp.pad(p["g_k_mla"][l][HEAD_DIM:], (HEAD_DIM, pad))
    g_out = p["g_out"][l]
    mla_w = N_MLA_HEADS * HEAD_DIM
    mem_w = N_MEM_HEADS * HEAD_DIM
    return {
        "g_mix": p["g_mix"][l][None, :],
        "w_in": w_in_p,
        "g_q_lat": p["g_q_lat"][l][None, :],
        "w_q": w_q,
        "g_kv_lat": p["g_kv_lat"][l][None, :],
        "w_k": w_k,
        "w_v_t": w_v.T,
        "gq_row": jnp.tile(gq * Q_SCALE, N_MLA_HEADS)[None, :],
        "gk_row": jnp.tile(gk, N_MLA_HEADS)[None, :],
        "gkpe_row": gkpe[None, :],
        "g_mem": p["g_mem"][l][None, :],
        "w_mem_kv": p["w_mem_kv"][l].astype(BF16),
        "gqm_row": jnp.tile(p["g_q_mem"][l], N_MEM_HEADS)[None, :],
        "gkm_row": jnp.tile(p["g_k_mem"][l], N_MEM_HEADS)[None, :],
        "conv_w": p["conv_w"][l],
        "g_out_mla": g_out[None, :mla_w],
        "g_out_mem": g_out[None, mla_w:mla_w + mem_w],
        "g_out_conv": g_out[None, mla_w + mem_w:],
        "w_out": p["w_out"][l].astype(BF16),
        "g_ffn": p["g_ffn"][l][None, :],
    }


def _tile_config(seq):
    return dict(tm=min(512, seq), tq=min(512, seq), tk=min(256, seq), tm_ffn=512,
                ffn_chunks=((0, 1536), (1536, 2816)), moe_chunks=((0, 1408),))


def _forward(p, *, tm, tq, tk, tm_ffn, ffn_chunks, moe_chunks):
    x = p["x"]
    batch, seq, d = x.shape
    depth = p["g_mix"].shape[0]
    assert seq % tm == 0 and seq % tq == 0 and (batch * seq) % tm_ffn == 0
    consts = _constants(seq)
    x2d = x.reshape(batch * seq, d)
    for l in range(depth):
        lw = _layer_weights(l, p)
        km, vm = _mem_kv(p["mem"], lw["g_mem"], lw["w_mem_kv"], consts["seg64"], lw["gkm_row"])
        q, k, vt, o_mem, o_conv = _proj(x2d, seq, tm, tk, lw, km, vm, consts["cos"], consts["sin"], consts)
        o_mla = _mla_attn(q, k, vt, batch, seq, tq)
        mix = (x2d, o_mla, o_mem, o_conv, lw["g_out_mla"], lw["w_out"])
        assert len(mix) == N_MIX_OPERANDS
        if l % 2 == 0:
            x2d = _dense_ffn(mix, lw["g_ffn"], p["w_dense_gu"][l // 2].astype(BF16),
                             p["w_dense_down"][l // 2].astype(BF16), tm_ffn, ffn_chunks)
        else:
            w_router = jnp.pad(p["w_router"][l // 2], ((0, 0), (0, LANES - N_EXPERTS)))
            x2d = _out_proj(mix, tm)
            x2d = _moe_ffn(x2d, lw["g_ffn"], w_router, p["w_expert_gu"][l // 2].astype(BF16),
                           p["w_expert_down"][l // 2].astype(BF16), tm_ffn, tm_ffn, moe_chunks)
    return x2d.reshape(batch, seq, d)


def kernel(x, mem, g_mix, w_in, g_q_lat, w_q_up, g_kv_lat, w_kv_up, g_q_mla, g_k_mla, g_mem, w_mem_kv, g_q_mem,
           g_k_mem, conv_w, g_out, w_out, g_ffn, w_dense_gu, w_dense_down, w_router, w_expert_gu, w_expert_down):
    p = dict(x=x, mem=mem, g_mix=g_mix, w_in=w_in, g_q_lat=g_q_lat, w_q_up=w_q_up, g_kv_lat=g_kv_lat,
             w_kv_up=w_kv_up, g_q_mla=g_q_mla, g_k_mla=g_k_mla, g_mem=g_mem, w_mem_kv=w_mem_kv, g_q_mem=g_q_mem,
             g_k_mem=g_k_mem, conv_w=conv_w, g_out=g_out, w_out=w_out, g_ffn=g_ffn, w_dense_gu=w_dense_gu,
             w_dense_down=w_dense_down, w_router=w_router, w_expert_gu=w_expert_gu, w_expert_down=w_expert_down)
    return _forward(p, **_tile_config(x.shape[1]))
```

```python
import functools
import math

import jax
import jax.numpy as jnp
from jax import lax
from jax.experimental import pallas as pl
from jax.experimental.pallas import tpu as pltpu

F32 = jnp.float32
BF16 = jnp.bfloat16

EPS = 1e-6
ROPE_THETA = 10000.0
HEAD_DIM = 64
ROPE_DIM = 32
N_MLA_HEADS = 8
N_MEM_HEADS = 4
CONV_K = 3
N_EXPERTS = 8
LANES = 128
HEAD_PAD = LANES
PAIR = 2 * LANES
ATTN_HEADS = 8
CARRY_ROWS = 8
SUM_ROWS = 16
V_AUG = HEAD_DIM + SUM_ROWS
VMEM_LIMIT = 56 * 1024 * 1024

NT_DIMS = (((1,), (1,)), ((), ()))
Q_SCALE = math.log2(math.e) / math.sqrt(HEAD_DIM + ROPE_DIM)


def _rms(x, g):
    return x * lax.rsqrt(jnp.mean(x * x, axis=-1, keepdims=True) + EPS) * g


def _seg_sumsq(x, seg):
    w = seg.shape[0]
    x2 = (x * x).astype(BF16)
    parts = [jnp.dot(x2[:, c:c + w], seg, preferred_element_type=F32) for c in range(0, x.shape[1], w)]
    return parts[0] if len(parts) == 1 else jnp.concatenate(parts, axis=-1)


def _rope_head(y, cos, sin_signed, lane):
    up = pltpu.roll(y, LANES - ROPE_DIM // 2, 1)
    down = pltpu.roll(y, ROPE_DIM // 2, 1)
    swapped = jnp.where(lane < HEAD_DIM + ROPE_DIM // 2, up, down)
    return y * cos + swapped * sin_signed


def _mem_kv_kernel(mem_ref, g_mem_ref, w_ref, seg_ref, gk_ref, km_ref, vm_ref):
    width = km_ref.shape[-1]
    mn = _rms(mem_ref[...], g_mem_ref[...]).astype(BF16)
    kv = jnp.dot(mn, w_ref[...], preferred_element_type=F32)
    km = kv[:, :width]
    ss = _seg_sumsq(km, seg_ref[...])
    km_ref[...] = (km * lax.rsqrt(ss * (1.0 / HEAD_DIM) + EPS) * gk_ref[...]).astype(BF16)
    vm_ref[...] = kv[:, width:].astype(BF16)


def _mem_kv(mem, g_mem, w_mem_kv, seg64, gk_row):
    b, m, d = mem.shape
    width = w_mem_kv.shape[1] // 2
    const = lambda i: (0, 0)
    return pl.pallas_call(
        _mem_kv_kernel,
        grid=(b,),
        in_specs=[
            pl.BlockSpec((None, m, d), lambda i: (i, 0, 0)),
            pl.BlockSpec((1, d), const),
            pl.BlockSpec((d, 2 * width), const),
            pl.BlockSpec((PAIR, PAIR), const),
            pl.BlockSpec((1, width), const),
        ],
        out_specs=[
            pl.BlockSpec((None, m, width), lambda i: (i, 0, 0)),
            pl.BlockSpec((None, m, width), lambda i: (i, 0, 0)),
        ],
        out_shape=[jax.ShapeDtypeStruct((b, m, width), BF16)] * 2,
        compiler_params=pltpu.CompilerParams(dimension_semantics=("parallel",), vmem_limit_bytes=VMEM_LIMIT),
        name="mem_kv",
    )(mem, g_mem, w_mem_kv, seg64, gk_row)


def _proj_kernel(x_ref, g_mix_ref, w_in_ref, g_qlat_ref, w_q_ref, g_kvlat_ref, w_k_ref, w_v_ref, v_ones_ref,
                 seg_mla_ref, inv_mla_ref, gq_ref, gk_ref, gkpe_ref, cos_ref, sin_ref,
                 km_ref, vm_ref, seg64_ref, gqm_ref, conv_w_ref, g_omem_ref, g_oconv_ref,
                 q_ref, k_ref, v_ref, omem_ref, oconv_ref, cu_ref, *, tiles_per_seq):
    tm = x_ref.shape[0]
    q_rank = g_qlat_ref.shape[1]
    kv_rank = g_kvlat_ref.shape[1]
    mem_w = gqm_ref.shape[1]
    conv_c = conv_w_ref.shape[1]
    tk = v_ref.shape[-1]
    sub = cu_ref.shape[0] - CARRY_ROWS

    @pl.when(pl.program_id(0) % tiles_per_seq == 0)
    def _():
        cu_ref[0:CARRY_ROWS, :] = jnp.zeros((CARRY_ROWS, conv_c), F32)

    lane = lax.broadcasted_iota(jnp.int32, (sub, LANES), 1)
    seg_mla = seg_mla_ref[...]
    inv_mla = inv_mla_ref[...]
    km = km_ref[...]
    vm = vm_ref[...]
    lane_m = lax.broadcasted_iota(jnp.int32, (km.shape[0], LANES), 1)
    ones_rows = jnp.concatenate([v_ones_ref[...]] * (tk // LANES), axis=1)
    w = conv_w_ref[...]

    for part in range(tm // sub):
        rows = slice(part * sub, (part + 1) * sub)
        h = _rms(x_ref[rows, :], g_mix_ref[...]).astype(BF16)
        z = jnp.dot(h, w_in_ref[...], preferred_element_type=F32)
        o = 0
        q_lat = z[:, o:o + q_rank]; o += q_rank
        kv_lat = z[:, o:o + kv_rank]; o += kv_rank
        kpe = z[:, o:o + HEAD_PAD]; o += HEAD_PAD
        q_mem = z[:, o:o + mem_w]; o += mem_w
        gate_b = z[:, o:o + conv_c]; o += conv_c
        gate_c = z[:, o:o + conv_c]; o += conv_c
        u = z[:, o:o + conv_c]
        cos = cos_ref[rows, :]
        sin = sin_ref[rows, :]

        qn = _rms(q_lat, g_qlat_ref[...]).astype(BF16)
        qf = jnp.dot(qn, w_q_ref[...], preferred_element_type=F32)
        n_rep = qf.shape[1] // PAIR
        inv_row = jnp.concatenate([inv_mla] * n_rep, axis=-1)
        qy = qf * lax.rsqrt(_seg_sumsq(qf, seg_mla) * inv_row + EPS) * gq_ref[...]
        for hd in range(qf.shape[1] // HEAD_PAD):
            sl = slice(hd * HEAD_PAD, (hd + 1) * HEAD_PAD)
            q_ref[rows, sl] = _rope_head(qy[:, sl], cos, sin, lane).astype(BF16)

        kvn = _rms(kv_lat, g_kvlat_ref[...]).astype(BF16)
        kf = jnp.dot(kvn, w_k_ref[...], preferred_element_type=F32)
        vt = lax.dot_general(w_v_ref[...], kvn, NT_DIMS, preferred_element_type=F32)
        for c in range(sub // tk):
            v_ref[part * (sub // tk) + c] = (vt[:, c * tk:(c + 1) * tk] + ones_rows).astype(BF16)
        ky = kf * lax.rsqrt(_seg_sumsq(kf, seg_mla) * inv_row + EPS) * gk_ref[...]
        kpe_y = kpe * lax.rsqrt(_seg_sumsq(kpe, seg_mla[:LANES, :LANES]) * inv_mla[:, :LANES] + EPS) * gkpe_ref[...]
        kpe_r = _rope_head(kpe_y, cos, sin, lane)
        for hd in range(kf.shape[1] // HEAD_PAD):
            sl = slice(hd * HEAD_PAD, (hd + 1) * HEAD_PAD)
            k_ref[rows, sl] = (ky[:, sl] + kpe_r).astype(BF16)

        qm = q_mem * lax.rsqrt(_seg_sumsq(q_mem, seg64_ref[...]) * (1.0 / HEAD_DIM) + EPS) * gqm_ref[...]
        groups = []
        for grp in range(mem_w // LANES):
            sl = slice(grp * LANES, (grp + 1) * LANES)
            qg, kg, vg = qm[:, sl], km[:, sl], vm[:, sl]
            og = None
            for half in range(LANES // HEAD_DIM):
                in_head = (lane >= half * HEAD_DIM) & (lane < (half + 1) * HEAD_DIM)
                in_head_m = (lane_m >= half * HEAD_DIM) & (lane_m < (half + 1) * HEAD_DIM)
                qh = jnp.where(in_head, qg, 0.0).astype(BF16)
                s = lax.dot_general(qh, kg, NT_DIMS, preferred_element_type=F32) * (1.0 / math.sqrt(HEAD_DIM))
                p = jnp.exp(s - jnp.max(s, axis=-1, keepdims=True))
                p = (p * (1.0 / jnp.sum(p, axis=-1, keepdims=True))).astype(BF16)
                vh = jnp.where(in_head_m, vg, jnp.zeros_like(vg))
                oh = jnp.dot(p, vh, preferred_element_type=F32)
                og = oh if og is None else og + oh
            groups.append(og)
        o_mem = jnp.concatenate(groups, axis=-1)
        omem_ref[rows, :] = _rms(o_mem, g_omem_ref[...]).astype(BF16)

        cu = gate_c * u
        cu_ref[CARRY_ROWS:CARRY_ROWS + sub, :] = cu
        conv = w[CONV_K - 1:CONV_K] * cu
        for tap in range(CONV_K - 1):
            shift = CONV_K - 1 - tap
            conv = conv + w[tap:tap + 1] * cu_ref[CARRY_ROWS - shift:CARRY_ROWS - shift + sub, :]
        oconv_ref[rows, :] = _rms(gate_b * conv, g_oconv_ref[...]).astype(BF16)
        cu_ref[0:CARRY_ROWS, :] = cu_ref[sub:sub + CARRY_ROWS, :]


def _proj(x2d, seq, tm, sub, tk, lw, km, vm, cos, sin, consts):
    t, d = x2d.shape
    tiles_per_seq = seq // tm
    n_q = lw["w_q"].shape[1]
    n_v = lw["w_v_t"].shape[0]
    mem_w = km.shape[-1]
    conv_c = lw["conv_w"].shape[1]
    const = lambda i: (0, 0)
    row = lambda i: (i, 0)
    full = lambda a: pl.BlockSpec(a.shape, const)
    operands = [
        (x2d, pl.BlockSpec((tm, d), row)),
        (lw["g_mix"], None), (lw["w_in"], None), (lw["g_q_lat"], None), (lw["w_q"], None),
        (lw["g_kv_lat"], None), (lw["w_k"], None), (lw["w_v_t"], None), (consts["v_ones"], None),
        (consts["seg_mla"], None), (consts["inv_mla"], None),
        (lw["gq_row"], None), (lw["gk_row"], None), (lw["gkpe_row"], None),
        (cos, pl.BlockSpec((tm, LANES), lambda i: (i % tiles_per_seq, 0))),
        (sin, pl.BlockSpec((tm, LANES), lambda i: (i % tiles_per_seq, 0))),
        (km, pl.BlockSpec((None,) + km.shape[1:], lambda i: (i // tiles_per_seq, 0, 0))),
        (vm, pl.BlockSpec((None,) + vm.shape[1:], lambda i: (i // tiles_per_seq, 0, 0))),
        (consts["seg64"], None), (lw["gqm_row"], None), (lw["conv_w"], None),
        (lw["g_out_mem"], None), (lw["g_out_conv"], None),
    ]
    args = [a for a, _ in operands]
    specs = [full(a) if s is None else s for a, s in operands]
    out_widths = (n_q, n_q, mem_w, conv_c)
    row_out = [(pl.BlockSpec((tm, w), row), jax.ShapeDtypeStruct((t, w), BF16)) for w in out_widths]
    vt_out = (pl.BlockSpec((tm // tk, n_v, tk), lambda i: (i, 0, 0)), jax.ShapeDtypeStruct((t // tk, n_v, tk), BF16))
    outs = row_out[:2] + [vt_out] + row_out[2:]
    return pl.pallas_call(
        functools.partial(_proj_kernel, tiles_per_seq=tiles_per_seq),
        grid=(t // tm,),
        in_specs=specs,
        out_specs=[spec for spec, _ in outs],
        out_shape=[shape for _, shape in outs],
        scratch_shapes=[pltpu.VMEM((sub + CARRY_ROWS, conv_c), F32)],
        compiler_params=pltpu.CompilerParams(dimension_semantics=("arbitrary",), vmem_limit_bytes=VMEM_LIMIT),
        name="proj",
    )(*args)


def _mla_attn_kernel(q_ref, k_ref, vt_ref, o_ref, s_scr):
    tq = q_ref.shape[0]
    tk = vt_ref.shape[-1]
    assert tq == 2 * tk and s_scr.shape[0] == 2
    qi = pl.program_id(2)
    neg = jnp.finfo(F32).min
    visible = lax.broadcasted_iota(jnp.int32, (tk, tk), 0) <= lax.broadcasted_iota(jnp.int32, (tk, tk), 1)

    n_heads = q_ref.shape[1] // HEAD_PAD
    head_lanes = [slice(hh * HEAD_PAD, (hh + 1) * HEAD_PAD) for hh in range(n_heads)]
    head_rows = [slice(hh * V_AUG, (hh + 1) * V_AUG) for hh in range(n_heads)]
    qs = [q_ref[:, sl] for sl in head_lanes]

    def key_block(j, hh):
        return k_ref[pl.ds(pl.multiple_of(j * tk, tk), tk), head_lanes[hh]]

    def scores_into(slot, j):
        for hh in range(n_heads):
            s_scr[slot, hh] = lax.dot_general(key_block(j, hh), qs[hh], NT_DIMS, preferred_element_type=F32)

    def update(state, s, vt_h):
        m, acc = state
        m_new = jnp.maximum(m, jnp.max(s, axis=0, keepdims=True))
        p = jnp.exp2(s - m_new).astype(BF16)
        return m_new, jnp.exp2(m - m_new) * acc + jnp.dot(vt_h, p, preferred_element_type=F32)

    def consume(slot, j, carry):
        vt = vt_ref[j]
        return tuple(update(carry[hh], s_scr[slot, hh], vt[head_rows[hh], :]) for hh in range(n_heads))

    def pair(i, carry):
        scores_into(1, 2 * i + 1)
        carry = consume(0, 2 * i, carry)
        scores_into(0, 2 * i + 2)
        return consume(1, 2 * i + 1, carry)

    init = tuple((jnp.full((1, tq), -jnp.inf, F32), jnp.zeros((V_AUG, tq), F32)) for _ in range(n_heads))
    scores_into(0, 0)
    carry = lax.fori_loop(0, qi, pair, init)

    vt_a = vt_ref[2 * qi]
    vt_b = vt_ref[2 * qi + 1]
    outs = []
    for hh in range(n_heads):
        m, acc = carry[hh]
        s_a = s_scr[0, hh]
        s_b = lax.dot_general(key_block(2 * qi + 1, hh), qs[hh][tk:, :], NT_DIMS, preferred_element_type=F32)
        va, vb = vt_a[head_rows[hh], :], vt_b[head_rows[hh], :]
        _, acc_l = update((m[:, :tk], acc[:, :tk]), jnp.where(visible, s_a[:, :tk], neg), va)
        right = update((m[:, tk:], acc[:, tk:]), s_a[:, tk:], va)
        _, acc_r = update(right, jnp.where(visible, s_b, neg), vb)
        acc = jnp.concatenate([acc_l, acc_r], axis=1)
        outs.append(acc[:HEAD_DIM] * (1.0 / acc[HEAD_DIM:HEAD_DIM + 1]))
    o_ref[...] = jnp.concatenate(outs, axis=0).T.astype(BF16)


def _mla_attn(q, k, vt, batch, seq, tq):
    t = q.shape[0]
    group = ATTN_HEADS * HEAD_PAD
    n_pairs = q.shape[1] // group
    nq = seq // tq
    _, v_rows, tk = vt.shape
    nk = seq // tk
    return pl.pallas_call(
        _mla_attn_kernel,
        grid=(batch, n_pairs, nq),
        in_specs=[
            pl.BlockSpec((tq, group), lambda b, hp, i: (b * nq + i, hp)),
            pl.BlockSpec((seq, group), lambda b, hp, i: (b, hp)),
            pl.BlockSpec((nk, v_rows // n_pairs, tk), lambda b, hp, i: (b, hp, 0)),
        ],
        out_specs=pl.BlockSpec((tq, ATTN_HEADS * HEAD_DIM), lambda b, hp, i: (b * nq + i, hp)),
        out_shape=jax.ShapeDtypeStruct((t, n_pairs * ATTN_HEADS * HEAD_DIM), BF16),
        scratch_shapes=[pltpu.VMEM((2, ATTN_HEADS, tk, tq), F32)],
        compiler_params=pltpu.CompilerParams(dimension_semantics=("parallel", "parallel", "parallel"),
                                             vmem_limit_bytes=VMEM_LIMIT),
        name="mla_attn",
    )(q, k, vt)


N_MIX_OPERANDS = 6


def _mixer_residual(x_ref, omla_ref, omem_ref, oconv_ref, g_ref, w_ref):
    a = _rms(omla_ref[...].astype(F32), g_ref[...]).astype(BF16)
    cat = jnp.concatenate([a, omem_ref[...], oconv_ref[...]], axis=-1)
    return x_ref[...] + jnp.dot(cat, w_ref[...], preferred_element_type=F32)


def _mixer_specs(mix, tm):
    row = lambda i: (i, 0)
    const = lambda i: (0, 0)
    return ([pl.BlockSpec((tm, a.shape[1]), row) for a in mix[:4]]
            + [pl.BlockSpec(mix[4].shape, const), pl.BlockSpec(mix[5].shape, const, pipeline_mode=pl.Buffered(1))])


def _out_proj_kernel(*refs):
    refs[N_MIX_OPERANDS][...] = _mixer_residual(*refs[:N_MIX_OPERANDS])


def _out_proj(mix, tm):
    t, d = mix[0].shape
    return pl.pallas_call(
        _out_proj_kernel,
        grid=(t // tm,),
        in_specs=_mixer_specs(mix, tm),
        out_specs=pl.BlockSpec((tm, d), lambda i: (i, 0)),
        out_shape=jax.ShapeDtypeStruct((t, d), F32),
        compiler_params=pltpu.CompilerParams(dimension_semantics=("parallel",), vmem_limit_bytes=VMEM_LIMIT),
        name="out_proj",
    )(*mix)


def _swiglu_part(h, wgu_ref, wd_ref, chunks):
    ff = wd_ref.shape[0]
    out = None
    for a, b in chunks:
        if (a, b) == (0, ff):
            gu = jnp.dot(h, wgu_ref[...], preferred_element_type=F32)
            g, u = gu[:, :ff], gu[:, ff:]
        else:
            g = jnp.dot(h, wgu_ref[:, a:b], preferred_element_type=F32)
            u = jnp.dot(h, wgu_ref[:, ff + a:ff + b], preferred_element_type=F32)
        act = (g * (1.0 / (1.0 + jnp.exp(-g))) * u).astype(BF16)
        part = jnp.dot(act, wd_ref[a:b, :], preferred_element_type=F32)
        out = part if out is None else out + part
    return out


def _dense_ffn_kernel(*refs, chunks):
    mix_refs, (g_ref, wgu_ref, wd_ref, y_ref) = refs[:N_MIX_OPERANDS], refs[N_MIX_OPERANDS:]
    x = _mixer_residual(*mix_refs)
    h = _rms(x, g_ref[...]).astype(BF16)
    y_ref[...] = x + _swiglu_part(h, wgu_ref, wd_ref, chunks)


def _dense_ffn(mix, g_ffn, wgu, wd, tm, chunks):
    t, d = mix[0].shape
    row = lambda i: (i, 0)
    const = lambda i: (0, 0)
    resident = lambda a: pl.BlockSpec(a.shape, const, pipeline_mode=pl.Buffered(1))
    return pl.pallas_call(
        functools.partial(_dense_ffn_kernel, chunks=chunks),
        grid=(t // tm,),
        in_specs=_mixer_specs(mix, tm) + [pl.BlockSpec(g_ffn.shape, const), resident(wgu), resident(wd)],
        out_specs=pl.BlockSpec((tm, d), row),
        out_shape=jax.ShapeDtypeStruct((t, d), F32),
        compiler_params=pltpu.CompilerParams(dimension_semantics=("parallel",), vmem_limit_bytes=VMEM_LIMIT),
        name="dense_ffn",
    )(*mix, g_ffn, wgu, wd)


ROUTE_ROWS = 8
TOP_K = 2
DMA_UNROLL = 8
ROW_TILE = 8


def _moe_route_kernel(x_ref, g_ref, wr_ref, tri_ref, rw_ref, ri_ref, cnt_ref, base_scr):
    tm = x_ref.shape[0]
    lane = lax.broadcasted_iota(jnp.int32, (tm, LANES), 1)
    lane_f = lane.astype(F32)

    @pl.when(pl.program_id(0) == 0)
    def _():
        base_scr[...] = jnp.zeros_like(base_scr)

    hf = _rms(x_ref[...], g_ref[...])
    wr = wr_ref[...]
    h_hi = hf.astype(BF16)
    h_lo = (hf - h_hi.astype(F32)).astype(BF16)
    w_hi = wr.astype(BF16)
    w_lo = (wr - w_hi.astype(F32)).astype(BF16)
    logits = (jnp.dot(h_hi, w_hi, preferred_element_type=F32) + jnp.dot(h_hi, w_lo, preferred_element_type=F32)
              + jnp.dot(h_lo, w_hi, preferred_element_type=F32))
    logits = jnp.where(lane < N_EXPERTS, logits, -jnp.inf)
    m1 = jnp.max(logits, axis=-1, keepdims=True)
    i1 = jnp.min(jnp.where(logits == m1, lane_f, float(LANES)), axis=-1, keepdims=True)
    rest = jnp.where(lane_f == i1, -jnp.inf, logits)
    m2 = jnp.max(rest, axis=-1, keepdims=True)
    i2 = jnp.min(jnp.where(rest == m2, lane_f, float(LANES)), axis=-1, keepdims=True)
    e2 = jnp.exp(m2 - m1)
    w1 = 1.0 / (1.0 + e2)
    rw_ref[...] = jnp.where(lane == 0, w1, jnp.where(lane == 1, e2 * w1, 0.0))

    chosen = jnp.where((lane_f == i1) | (lane_f == i2), 1.0, 0.0)
    before = jnp.dot(tri_ref[...], chosen.astype(BF16), preferred_element_type=F32) + base_scr[...]
    r1 = jnp.sum(jnp.where(lane_f == i1, before, 0.0), axis=-1, keepdims=True)
    r2 = jnp.sum(jnp.where(lane_f == i2, before, 0.0), axis=-1, keepdims=True)
    base_scr[...] += jnp.sum(chosen, axis=0, keepdims=True)
    cnt_ref[...] = base_scr[...].astype(jnp.int32)
    table = jnp.where(lane == 0, i1, jnp.where(lane == 1, i2, jnp.where(lane == 2, r1, jnp.where(lane == 3, r2, 0.0))))
    ri_ref[...] = table.T[:ROUTE_ROWS, :].astype(jnp.int32)


def _moe_route(x2d, g_ffn, w_router, tm):
    t, d = x2d.shape
    tri = (jnp.arange(tm)[:, None] > jnp.arange(tm)[None, :]).astype(BF16)
    const = lambda i: (0, 0)
    return pl.pallas_call(
        _moe_route_kernel,
        grid=(t // tm,),
        in_specs=[pl.BlockSpec((tm, d), lambda i: (i, 0)), pl.BlockSpec(g_ffn.shape, const),
                  pl.BlockSpec(w_router.shape, const), pl.BlockSpec((tm, tm), const)],
        out_specs=[pl.BlockSpec((tm, LANES), lambda i: (i, 0)),
                   pl.BlockSpec((None, ROUTE_ROWS, tm), lambda i: (i, 0, 0)),
                   pl.BlockSpec((1, LANES), const)],
        out_shape=[jax.ShapeDtypeStruct((t, LANES), F32),
                   jax.ShapeDtypeStruct((t // tm, ROUTE_ROWS, tm), jnp.int32),
                   jax.ShapeDtypeStruct((1, LANES), jnp.int32)],
        scratch_shapes=[pltpu.VMEM((1, LANES), F32)],
        compiler_params=pltpu.CompilerParams(dimension_semantics=("arbitrary",), vmem_limit_bytes=VMEM_LIMIT),
        name="moe_route",
    )(x2d, g_ffn, w_router, tri)


def _token_copy(src_ref, src_tok, dst_ref, dst_tok, sem):
    src = src_ref.at[pl.ds(pl.multiple_of(src_tok * ROW_TILE, ROW_TILE), ROW_TILE)]
    dst = dst_ref.at[pl.ds(pl.multiple_of(dst_tok * ROW_TILE, ROW_TILE), ROW_TILE)]
    return pltpu.make_async_copy(src, dst, sem)


def _to_token_tiles(ref, value):
    rows = value.shape[0]
    for c in range(ROW_TILE):
        ref[pl.ds(c, rows, stride=ROW_TILE), :] = value[:, c * LANES:(c + 1) * LANES]


def _from_token_tiles(ref, rows):
    return jnp.concatenate([ref[pl.ds(c, rows, stride=ROW_TILE), :] for c in range(ROW_TILE)], axis=-1)


def _moe_dispatch_kernel(zlo_ref, zhi_ref, x_ref, g_ref, pos_ref, xs_ref, h_scr, zero_scr, sem):
    tm = x_ref.shape[0]
    _to_token_tiles(h_scr, _rms(x_ref[...], g_ref[...]))

    def start(r, c):
        for k in range(TOP_K):
            _token_copy(h_scr, r, xs_ref, pos_ref[0, k * tm + r], sem).start(priority=k)
        return c

    def wait(r, c):
        for k in range(TOP_K):
            _token_copy(h_scr, 0, xs_ref, 0, sem).wait()
        return c

    lax.fori_loop(0, tm, start, 0, unroll=DMA_UNROLL)
    lax.fori_loop(0, tm, wait, 0, unroll=DMA_UNROLL)

    @pl.when(pl.program_id(0) == pl.num_programs(0) - 1)
    def _():
        zero_scr[...] = jnp.zeros_like(zero_scr)
        for e in range(N_EXPERTS):
            def zstart(r, c):
                _token_copy(zero_scr, 0, xs_ref, r, sem).start()
                return c

            def zwait(r, c):
                _token_copy(zero_scr, 0, xs_ref, 0, sem).wait()
                return c

            lax.fori_loop(zlo_ref[e], zhi_ref[e], zstart, 0)
            lax.fori_loop(zlo_ref[e], zhi_ref[e], zwait, 0)


def _moe_dispatch(x2d, g_ffn, pos, zero_lo, zero_hi, n_rows, tm):
    t, d = x2d.shape
    assert d == ROW_TILE * LANES
    return pl.pallas_call(
        _moe_dispatch_kernel,
        grid_spec=pltpu.PrefetchScalarGridSpec(
            num_scalar_prefetch=2,
            grid=(t // tm,),
            in_specs=[pl.BlockSpec((tm, d), lambda i, *_: (i, 0)),
                      pl.BlockSpec(g_ffn.shape, lambda i, *_: (0, 0)),
                      pl.BlockSpec((None, 1, TOP_K * tm), lambda i, *_: (i, 0, 0), memory_space=pltpu.SMEM)],
            out_specs=pl.BlockSpec(memory_space=pl.ANY),
            scratch_shapes=[pltpu.VMEM((tm * ROW_TILE, LANES), F32), pltpu.VMEM((ROW_TILE, LANES), F32),
                            pltpu.SemaphoreType.DMA(())],
        ),
        out_shape=jax.ShapeDtypeStruct((n_rows * ROW_TILE, LANES), F32),
        compiler_params=pltpu.CompilerParams(dimension_semantics=("arbitrary",), vmem_limit_bytes=VMEM_LIMIT),
        name="moe_dispatch",
    )(zero_lo, zero_hi, x2d, g_ffn, pos)


def _moe_gmm_kernel(te_ref, nu_ref, xs_ref, wgu_ref, wd_ref, ys_ref, *, chunks):
    del te_ref
    tg = xs_ref.shape[0] // ROW_TILE
    used = pl.program_id(0) < nu_ref[0]

    @pl.when(used)
    def _():
        x = _from_token_tiles(xs_ref, tg).astype(BF16)
        _to_token_tiles(ys_ref, _swiglu_part(x, wgu_ref, wd_ref, chunks))

    @pl.when(jnp.logical_not(used))
    def _():
        ys_ref[...] = jnp.zeros_like(ys_ref)


def _moe_gmm(xs, tile_expert, n_used, wgu, wd, tg, chunks):
    _, ff, d = wd.shape
    n_tiles = xs.shape[0] // (tg * ROW_TILE)
    row = lambda i, te, nu: (i, 0)
    by_expert = lambda i, te, nu: (te[i], 0, 0)
    return pl.pallas_call(
        functools.partial(_moe_gmm_kernel, chunks=chunks),
        grid_spec=pltpu.PrefetchScalarGridSpec(
            num_scalar_prefetch=2,
            grid=(n_tiles,),
            in_specs=[pl.BlockSpec((tg * ROW_TILE, LANES), row),
                      pl.BlockSpec((None, d, 2 * ff), by_expert),
                      pl.BlockSpec((None, ff, d), by_expert)],
            out_specs=pl.BlockSpec((tg * ROW_TILE, LANES), row),
        ),
        out_shape=jax.ShapeDtypeStruct(xs.shape, F32),
        compiler_params=pltpu.CompilerParams(dimension_semantics=("arbitrary",), vmem_limit_bytes=VMEM_LIMIT),
        name="moe_gmm",
    )(tile_expert, n_used, xs, wgu, wd)


def _moe_combine_kernel(x_ref, rw_ref, pos_ref, ys_ref, y_ref, a_scr, b_scr, sem):
    tm = x_ref.shape[0]
    bufs = (a_scr, b_scr)

    def start(r, c):
        for k in range(TOP_K):
            _token_copy(ys_ref, pos_ref[0, k * tm + r], bufs[k], r, sem).start(priority=k)
        return c

    def wait(r, c):
        for k in range(TOP_K):
            _token_copy(ys_ref, 0, bufs[k], 0, sem).wait()
        return c

    lax.fori_loop(0, tm, start, 0, unroll=DMA_UNROLL)
    lax.fori_loop(0, tm, wait, 0, unroll=DMA_UNROLL)
    rw = rw_ref[...]
    y_ref[...] = x_ref[...] + rw[:, 0:1] * _from_token_tiles(a_scr, tm) + rw[:, 1:2] * _from_token_tiles(b_scr, tm)


def _moe_combine(x2d, route_w, pos, ys, tm):
    t, d = x2d.shape
    return pl.pallas_call(
        _moe_combine_kernel,
        grid=(t // tm,),
        in_specs=[pl.BlockSpec((tm, d), lambda i: (i, 0)),
                  pl.BlockSpec((tm, LANES), lambda i: (i, 0)),
                  pl.BlockSpec((None, 1, TOP_K * tm), lambda i: (i, 0, 0), memory_space=pltpu.SMEM),
                  pl.BlockSpec(memory_space=pl.ANY)],
        out_specs=pl.BlockSpec((tm, d), lambda i: (i, 0)),
        out_shape=jax.ShapeDtypeStruct((t, d), F32),
        scratch_shapes=[pltpu.VMEM((tm * ROW_TILE, LANES), F32), pltpu.VMEM((tm * ROW_TILE, LANES), F32),
                        pltpu.SemaphoreType.DMA(())],
        compiler_params=pltpu.CompilerParams(dimension_semantics=("arbitrary",), vmem_limit_bytes=VMEM_LIMIT),
        name="moe_combine",
    )(x2d, route_w, pos, ys)


def _moe_ffn(x2d, g_ffn, w_router, wgu, wd, tm, tg, chunks):
    t, _ = x2d.shape
    n_e = wd.shape[0]
    route_w, route_i, counts = _moe_route(x2d, g_ffn, w_router, tm)
    cnt = counts[0, :n_e]
    tiles = (cnt + tg - 1) // tg
    tile_end = jnp.cumsum(tiles)
    off = ((tile_end - tiles) * tg).astype(jnp.int32)
    n_tiles = (TOP_K * t) // tg + n_e
    tile_ids = jnp.arange(n_tiles, dtype=jnp.int32)
    tile_expert = jnp.sum((tile_ids[:, None] >= tile_end[None, :]).astype(jnp.int32), axis=1)
    tile_expert = jnp.minimum(tile_expert, n_e - 1).astype(jnp.int32)
    zero_hi = (tile_end * tg).astype(jnp.int32).at[n_e - 1].set(n_tiles * tg)
    pos = jnp.take(off, route_i[:, :TOP_K, :]) + route_i[:, TOP_K:2 * TOP_K, :]
    pos = pos.reshape(t // tm, 1, TOP_K * tm)
    xs = _moe_dispatch(x2d, g_ffn, pos, off + cnt, zero_hi, n_tiles * tg, tm)
    ys = _moe_gmm(xs, tile_expert, tile_end[-1:].astype(jnp.int32), wgu, wd, tg, chunks)
    return _moe_combine(x2d, route_w, pos, ys, tm)


def _segment_matrix(bounds):
    seg_id = jnp.zeros((bounds[-1],), jnp.int32)
    for b in bounds[1:-1]:
        seg_id = seg_id + (jnp.arange(bounds[-1]) >= b).astype(jnp.int32)
    return (seg_id[:, None] == seg_id[None, :]).astype(BF16)


def _constants(seq):
    head_bounds = [0, HEAD_DIM, HEAD_DIM + ROPE_DIM, HEAD_PAD]
    pair_bounds = head_bounds + [HEAD_PAD + b for b in head_bounds[1:]]
    inv_head = jnp.concatenate([jnp.full((HEAD_DIM,), 1.0 / HEAD_DIM, F32),
                                jnp.full((HEAD_PAD - HEAD_DIM,), 1.0 / ROPE_DIM, F32)])
    half = ROPE_DIM // 2
    inv = 1.0 / (ROPE_THETA ** (jnp.arange(0, ROPE_DIM, 2, dtype=F32) / ROPE_DIM))
    ang = jnp.arange(seq, dtype=F32)[:, None] * inv[None, :]
    ones = jnp.ones((seq, HEAD_DIM), F32)
    zeros = jnp.zeros((seq, HEAD_DIM), F32)
    pad1 = jnp.ones((seq, HEAD_PAD - HEAD_DIM - ROPE_DIM), F32)
    cos = jnp.concatenate([ones, jnp.cos(ang), jnp.cos(ang), pad1], axis=-1)
    sin = jnp.concatenate([zeros, -jnp.sin(ang), jnp.sin(ang), 0.0 * pad1], axis=-1)
    assert cos.shape == (seq, HEAD_PAD) and half * 2 == ROPE_DIM
    return {
        "seg_mla": _segment_matrix(pair_bounds),
        "inv_mla": jnp.concatenate([inv_head, inv_head])[None, :],
        "seg64": _segment_matrix(list(range(0, PAIR + 1, HEAD_DIM))),
        "v_ones": jnp.broadcast_to(
            jnp.tile(jnp.concatenate([jnp.zeros((HEAD_DIM,), F32), jnp.ones((SUM_ROWS,), F32)]), N_MLA_HEADS)[:, None],
            (N_MLA_HEADS * V_AUG, LANES)),
        "cos": cos,
        "sin": sin,
    }


def _layer_weights(l, p):
    d = p["w_in"].shape[1]
    q_rank = p["g_q_lat"].shape[1]
    kv_rank = p["g_kv_lat"].shape[1]
    qk_dim = HEAD_DIM + ROPE_DIM
    pad = HEAD_PAD - qk_dim
    w_in = p["w_in"][l]
    o = q_rank + kv_rank
    w_kpe = w_in[:, o:o + ROPE_DIM]
    w_in_p = jnp.concatenate([
        w_in[:, :o],
        jnp.zeros((d, HEAD_DIM), F32), w_kpe, jnp.zeros((d, pad), F32),
        w_in[:, o + ROPE_DIM:],
    ], axis=1).astype(BF16)

    w_q = p["w_q_up"][l].reshape(q_rank, N_MLA_HEADS, qk_dim)
    w_q = jnp.pad(w_q, ((0, 0), (0, 0), (0, pad))).reshape(q_rank, N_MLA_HEADS * HEAD_PAD).astype(BF16)
    w_kv = p["w_kv_up"][l].reshape(kv_rank, N_MLA_HEADS, 2 * HEAD_DIM)
    w_k = jnp.pad(w_kv[:, :, :HEAD_DIM], ((0, 0), (0, 0), (0, HEAD_PAD - HEAD_DIM)))
    w_k = w_k.reshape(kv_rank, N_MLA_HEADS * HEAD_PAD).astype(BF16)
    w_v = jnp.pad(w_kv[:, :, HEAD_DIM:], ((0, 0), (0, 0), (0, SUM_ROWS)))
    w_v = w_v.reshape(kv_rank, N_MLA_HEADS * V_AUG).astype(BF16)

    gq = jnp.pad(p["g_q_mla"][l], (0, pad))
    gk = jnp.pad(p["g_k_mla"][l][:HEAD_DIM], (0, HEAD_PAD - HEAD_DIM))
    gkpe = jnp.pad(p["g_k_mla"][l][HEAD_DIM:], (HEAD_DIM, pad))
    g_out = p["g_out"][l]
    mla_w = N_MLA_HEADS * HEAD_DIM
    mem_w = N_MEM_HEADS * HEAD_DIM
    return {
        "g_mix": p["g_mix"][l][None, :],
        "w_in": w_in_p,
        "g_q_lat": p["g_q_lat"][l][None, :],
        "w_q": w_q,
        "g_kv_lat": p["g_kv_lat"][l][None, :],
        "w_k": w_k,
        "w_v_t": w_v.T,
        "gq_row": jnp.tile(gq * Q_SCALE, N_MLA_HEADS)[None, :],
        "gk_row": jnp.tile(gk, N_MLA_HEADS)[None, :],
        "gkpe_row": gkpe[None, :],
        "g_mem": p["g_mem"][l][None, :],
        "w_mem_kv": p["w_mem_kv"][l].astype(BF16),
        "gqm_row": jnp.tile(p["g_q_mem"][l], N_MEM_HEADS)[None, :],
        "gkm_row": jnp.tile(p["g_k_mem"][l], N_MEM_HEADS)[None, :],
        "conv_w": p["conv_w"][l],
        "g_out_mla": g_out[None, :mla_w],
        "g_out_mem": g_out[None, mla_w:mla_w + mem_w],
        "g_out_conv": g_out[None, mla_w + mem_w:],
        "w_out": p["w_out"][l].astype(BF16),
        "g_ffn": p["g_ffn"][l][None, :],
    }


def _tile_config(seq):
    return dict(tm=min(1024, seq), proj_sub=min(512, seq), tq=min(512, seq), tk=min(256, seq), tm_ffn=512,
                ffn_chunks=((0, 1536), (1536, 2816)), moe_chunks=((0, 1408),))


def _forward(p, *, tm, proj_sub, tq, tk, tm_ffn, ffn_chunks, moe_chunks):
    x = p["x"]
    batch, seq, d = x.shape
    depth = p["g_mix"].shape[0]
    assert seq % tm == 0 and seq % tq == 0 and (batch * seq) % tm_ffn == 0
    consts = _constants(seq)
    x2d = x.reshape(batch * seq, d)
    for l in range(depth):
        lw = _layer_weights(l, p)
        km, vm = _mem_kv(p["mem"], lw["g_mem"], lw["w_mem_kv"], consts["seg64"], lw["gkm_row"])
        q, k, vt, o_mem, o_conv = _proj(x2d, seq, tm, proj_sub, tk, lw, km, vm, consts["cos"], consts["sin"], consts)
        o_mla = _mla_attn(q, k, vt, batch, seq, tq)
        mix = (x2d, o_mla, o_mem, o_conv, lw["g_out_mla"], lw["w_out"])
        assert len(mix) == N_MIX_OPERANDS
        if l % 2 == 0:
            x2d = _dense_ffn(mix, lw["g_ffn"], p["w_dense_gu"][l // 2].astype(BF16),
                             p["w_dense_down"][l // 2].astype(BF16), tm_ffn, ffn_chunks)
        else:
            w_router = jnp.pad(p["w_router"][l // 2], ((0, 0), (0, LANES - N_EXPERTS)))
            x2d = _out_proj(mix, tm)
            x2d = _moe_ffn(x2d, lw["g_ffn"], w_router, p["w_expert_gu"][l // 2].astype(BF16),
                           p["w_expert_down"][l // 2].astype(BF16), tm_ffn, tm_ffn, moe_chunks)
    return x2d.reshape(batch, seq, d)


def kernel(x, mem, g_mix, w_in, g_q_lat, w_q_up, g_kv_lat, w_kv_up, g_q_mla, g_k_mla, g_mem, w_mem_kv, g_q_mem,
           g_k_mem, conv_w, g_out, w_out, g_ffn, w_dense_gu, w_dense_down, w_router, w_expert_gu, w_expert_down):
    p = dict(x=x, mem=mem, g_mix=g_mix, w_in=w_in, g_q_lat=g_q_lat, w_q_up=w_q_up, g_kv_lat=g_kv_lat,
             w_kv_up=w_kv_up, g_q_mla=g_q_mla, g_k_mla=g_k_mla, g_mem=g_mem, w_mem_kv=w_mem_kv, g_q_mem=g_q_mem,
             g_k_mem=g_k_mem, conv_w=conv_w, g_out=g_out, w_out=w_out, g_ffn=g_ffn, w_dense_gu=w_dense_gu,
             w_dense_down=w_dense_down, w_router=w_router, w_expert_gu=w_expert_gu, w_expert_down=w_expert_down)
    return _forward(p, **_tile_config(x.shape[1]))
```

```python
import functools
import math

import jax
import jax.numpy as jnp
from jax import lax
from jax.experimental import pallas as pl
from jax.experimental.pallas import tpu as pltpu

F32 = jnp.float32
BF16 = jnp.bfloat16

EPS = 1e-6
ROPE_THETA = 10000.0
HEAD_DIM = 64
ROPE_DIM = 32
N_MLA_HEADS = 8
N_MEM_HEADS = 4
CONV_K = 3
N_EXPERTS = 8
LANES = 128
HEAD_PAD = LANES
PAIR = 2 * LANES
ATTN_HEADS = 8
CARRY_ROWS = 8
SUM_ROWS = 16
V_AUG = HEAD_DIM + SUM_ROWS
VMEM_LIMIT = 56 * 1024 * 1024

NT_DIMS = (((1,), (1,)), ((), ()))
Q_SCALE = math.log2(math.e) / math.sqrt(HEAD_DIM + ROPE_DIM)


def _rms(x, g):
    return x * lax.rsqrt(jnp.mean(x * x, axis=-1, keepdims=True) + EPS) * g


def _seg_sumsq(x, seg):
    w = seg.shape[0]
    x2 = (x * x).astype(BF16)
    parts = [jnp.dot(x2[:, c:c + w], seg, preferred_element_type=F32) for c in range(0, x.shape[1], w)]
    return parts[0] if len(parts) == 1 else jnp.concatenate(parts, axis=-1)


def _rope_head(y, cos, sin_signed, lane):
    up = pltpu.roll(y, LANES - ROPE_DIM // 2, 1)
    down = pltpu.roll(y, ROPE_DIM // 2, 1)
    swapped = jnp.where(lane < HEAD_DIM + ROPE_DIM // 2, up, down)
    return y * cos + swapped * sin_signed


def _mem_kv_kernel(mem_ref, g_mem_ref, w_ref, seg_ref, gk_ref, km_ref, vm_ref):
    width = km_ref.shape[-1]
    mn = _rms(mem_ref[...], g_mem_ref[...]).astype(BF16)
    kv = jnp.dot(mn, w_ref[...], preferred_element_type=F32)
    km = kv[:, :width]
    ss = _seg_sumsq(km, seg_ref[...])
    km_ref[...] = (km * lax.rsqrt(ss * (1.0 / HEAD_DIM) + EPS) * gk_ref[...]).astype(BF16)
    vm_ref[...] = kv[:, width:].astype(BF16)


def _mem_kv(mem, g_mem, w_mem_kv, seg64, gk_row):
    b, m, d = mem.shape
    width = w_mem_kv.shape[1] // 2
    const = lambda i: (0, 0)
    return pl.pallas_call(
        _mem_kv_kernel,
        grid=(b,),
        in_specs=[
            pl.BlockSpec((None, m, d), lambda i: (i, 0, 0)),
            pl.BlockSpec((1, d), const),
            pl.BlockSpec((d, 2 * width), const),
            pl.BlockSpec((PAIR, PAIR), const),
            pl.BlockSpec((1, width), const),
        ],
        out_specs=[
            pl.BlockSpec((None, m, width), lambda i: (i, 0, 0)),
            pl.BlockSpec((None, m, width), lambda i: (i, 0, 0)),
        ],
        out_shape=[jax.ShapeDtypeStruct((b, m, width), BF16)] * 2,
        compiler_params=pltpu.CompilerParams(dimension_semantics=("parallel",), vmem_limit_bytes=VMEM_LIMIT),
        name="mem_kv",
    )(mem, g_mem, w_mem_kv, seg64, gk_row)


def _proj_kernel(x_ref, g_mix_ref, w_in_ref, g_qlat_ref, w_q_ref, g_kvlat_ref, w_k_ref, w_v_ref, v_ones_ref,
                 seg_mla_ref, inv_mla_ref, gq_ref, gk_ref, gkpe_ref, cos_ref, sin_ref,
                 km_ref, vm_ref, seg64_ref, gqm_ref, conv_w_ref, g_omem_ref, g_oconv_ref,
                 q_ref, k_ref, v_ref, omem_ref, oconv_ref, cu_ref, *, tiles_per_seq):
    tm = x_ref.shape[0]
    q_rank = g_qlat_ref.shape[1]
    kv_rank = g_kvlat_ref.shape[1]
    mem_w = gqm_ref.shape[1]
    conv_c = conv_w_ref.shape[1]
    tk = v_ref.shape[-1]
    sub = cu_ref.shape[0] - CARRY_ROWS

    @pl.when(pl.program_id(0) % tiles_per_seq == 0)
    def _():
        cu_ref[0:CARRY_ROWS, :] = jnp.zeros((CARRY_ROWS, conv_c), F32)

    lane = lax.broadcasted_iota(jnp.int32, (sub, LANES), 1)
    seg_mla = seg_mla_ref[...]
    inv_mla = inv_mla_ref[...]
    km = km_ref[...]
    vm = vm_ref[...]
    lane_m = lax.broadcasted_iota(jnp.int32, (km.shape[0], LANES), 1)
    ones_rows = jnp.concatenate([v_ones_ref[...]] * (tk // LANES), axis=1)
    w = conv_w_ref[...]

    for part in range(tm // sub):
        rows = slice(part * sub, (part + 1) * sub)
        h = _rms(x_ref[rows, :], g_mix_ref[...]).astype(BF16)
        z = jnp.dot(h, w_in_ref[...], preferred_element_type=F32)
        o = 0
        q_lat = z[:, o:o + q_rank]; o += q_rank
        kv_lat = z[:, o:o + kv_rank]; o += kv_rank
        kpe = z[:, o:o + HEAD_PAD]; o += HEAD_PAD
        q_mem = z[:, o:o + mem_w]; o += mem_w
        gate_b = z[:, o:o + conv_c]; o += conv_c
        gate_c = z[:, o:o + conv_c]; o += conv_c
        u = z[:, o:o + conv_c]
        cos = cos_ref[rows, :]
        sin = sin_ref[rows, :]

        qn = _rms(q_lat, g_qlat_ref[...]).astype(BF16)
        qf = jnp.dot(qn, w_q_ref[...], preferred_element_type=F32)
        n_rep = qf.shape[1] // PAIR
        inv_row = jnp.concatenate([inv_mla] * n_rep, axis=-1)
        qy = qf * lax.rsqrt(_seg_sumsq(qf, seg_mla) * inv_row + EPS) * gq_ref[...]
        for hd in range(qf.shape[1] // HEAD_PAD):
            sl = slice(hd * HEAD_PAD, (hd + 1) * HEAD_PAD)
            q_ref[rows, sl] = _rope_head(qy[:, sl], cos, sin, lane).astype(BF16)

        kvn = _rms(kv_lat, g_kvlat_ref[...]).astype(BF16)
        kf = jnp.dot(kvn, w_k_ref[...], preferred_element_type=F32)
        vt = lax.dot_general(w_v_ref[...], kvn, NT_DIMS, preferred_element_type=F32)
        for c in range(sub // tk):
            v_ref[part * (sub // tk) + c] = (vt[:, c * tk:(c + 1) * tk] + ones_rows).astype(BF16)
        ky = kf * lax.rsqrt(_seg_sumsq(kf, seg_mla) * inv_row + EPS) * gk_ref[...]
        kpe_y = kpe * lax.rsqrt(_seg_sumsq(kpe, seg_mla[:LANES, :LANES]) * inv_mla[:, :LANES] + EPS) * gkpe_ref[...]
        kpe_r = _rope_head(kpe_y, cos, sin, lane)
        for hd in range(kf.shape[1] // HEAD_PAD):
            sl = slice(hd * HEAD_PAD, (hd + 1) * HEAD_PAD)
            k_ref[rows, sl] = (ky[:, sl] + kpe_r).astype(BF16)

        qm = q_mem * lax.rsqrt(_seg_sumsq(q_mem, seg64_ref[...]) * (1.0 / HEAD_DIM) + EPS) * gqm_ref[...]
        groups = []
        for grp in range(mem_w // LANES):
            sl = slice(grp * LANES, (grp + 1) * LANES)
            qg, kg, vg = qm[:, sl], km[:, sl], vm[:, sl]
            og = None
            for half in range(LANES // HEAD_DIM):
                in_head = (lane >= half * HEAD_DIM) & (lane < (half + 1) * HEAD_DIM)
                in_head_m = (lane_m >= half * HEAD_DIM) & (lane_m < (half + 1) * HEAD_DIM)
                qh = jnp.where(in_head, qg, 0.0).astype(BF16)
                s = lax.dot_general(qh, kg, NT_DIMS, preferred_element_type=F32) * (1.0 / math.sqrt(HEAD_DIM))
                p = jnp.exp(s - jnp.max(s, axis=-1, keepdims=True))
                p = (p * (1.0 / jnp.sum(p, axis=-1, keepdims=True))).astype(BF16)
                vh = jnp.where(in_head_m, vg, jnp.zeros_like(vg))
                oh = jnp.dot(p, vh, preferred_element_type=F32)
                og = oh if og is None else og + oh
            groups.append(og)
        o_mem = jnp.concatenate(groups, axis=-1)
        omem_ref[rows, :] = _rms(o_mem, g_omem_ref[...]).astype(BF16)

        cu = gate_c * u
        cu_ref[CARRY_ROWS:CARRY_ROWS + sub, :] = cu
        conv = w[CONV_K - 1:CONV_K] * cu
        for tap in range(CONV_K - 1):
            shift = CONV_K - 1 - tap
            conv = conv + w[tap:tap + 1] * cu_ref[CARRY_ROWS - shift:CARRY_ROWS - shift + sub, :]
        oconv_ref[rows, :] = _rms(gate_b * conv, g_oconv_ref[...]).astype(BF16)
        cu_ref[0:CARRY_ROWS, :] = cu_ref[sub:sub + CARRY_ROWS, :]


def _proj(x2d, seq, tm, sub, tk, lw, km, vm, cos, sin, consts):
    t, d = x2d.shape
    tiles_per_seq = seq // tm
    n_q = lw["w_q"].shape[1]
    n_v = lw["w_v_t"].shape[0]
    mem_w = km.shape[-1]
    conv_c = lw["conv_w"].shape[1]
    const = lambda i: (0, 0)
    row = lambda i: (i, 0)
    full = lambda a: pl.BlockSpec(a.shape, const)
    operands = [
        (x2d, pl.BlockSpec((tm, d), row)),
        (lw["g_mix"], None), (lw["w_in"], None), (lw["g_q_lat"], None), (lw["w_q"], None),
        (lw["g_kv_lat"], None), (lw["w_k"], None), (lw["w_v_t"], None), (consts["v_ones"], None),
        (consts["seg_mla"], None), (consts["inv_mla"], None),
        (lw["gq_row"], None), (lw["gk_row"], None), (lw["gkpe_row"], None),
        (cos, pl.BlockSpec((tm, LANES), lambda i: (i % tiles_per_seq, 0))),
        (sin, pl.BlockSpec((tm, LANES), lambda i: (i % tiles_per_seq, 0))),
        (km, pl.BlockSpec((None,) + km.shape[1:], lambda i: (i // tiles_per_seq, 0, 0))),
        (vm, pl.BlockSpec((None,) + vm.shape[1:], lambda i: (i // tiles_per_seq, 0, 0))),
        (consts["seg64"], None), (lw["gqm_row"], None), (lw["conv_w"], None),
        (lw["g_out_mem"], None), (lw["g_out_conv"], None),
    ]
    args = [a for a, _ in operands]
    specs = [full(a) if s is None else s for a, s in operands]
    out_widths = (n_q, n_q, mem_w, conv_c)
    row_out = [(pl.BlockSpec((tm, w), row), jax.ShapeDtypeStruct((t, w), BF16)) for w in out_widths]
    vt_out = (pl.BlockSpec((tm // tk, n_v, tk), lambda i: (i, 0, 0)), jax.ShapeDtypeStruct((t // tk, n_v, tk), BF16))
    outs = row_out[:2] + [vt_out] + row_out[2:]
    return pl.pallas_call(
        functools.partial(_proj_kernel, tiles_per_seq=tiles_per_seq),
        grid=(t // tm,),
        in_specs=specs,
        out_specs=[spec for spec, _ in outs],
        out_shape=[shape for _, shape in outs],
        scratch_shapes=[pltpu.VMEM((sub + CARRY_ROWS, conv_c), F32)],
        compiler_params=pltpu.CompilerParams(dimension_semantics=("arbitrary",), vmem_limit_bytes=VMEM_LIMIT),
        name="proj",
    )(*args)


def _mla_attn_kernel(q_ref, k_ref, vt_ref, o_ref, s_scr):
    tq = q_ref.shape[0]
    tk = vt_ref.shape[-1]
    assert tq == 2 * tk and s_scr.shape[0] == 2
    qi = pl.program_id(2)
    neg = jnp.finfo(F32).min
    visible = lax.broadcasted_iota(jnp.int32, (tk, tk), 0) <= lax.broadcasted_iota(jnp.int32, (tk, tk), 1)

    n_heads = q_ref.shape[1] // HEAD_PAD
    head_lanes = [slice(hh * HEAD_PAD, (hh + 1) * HEAD_PAD) for hh in range(n_heads)]
    head_rows = [slice(hh * V_AUG, (hh + 1) * V_AUG) for hh in range(n_heads)]
    qs = [q_ref[:, sl] for sl in head_lanes]

    def key_block(j, hh):
        return k_ref[pl.ds(pl.multiple_of(j * tk, tk), tk), head_lanes[hh]]

    def scores_into(slot, j):
        for hh in range(n_heads):
            s_scr[slot, hh] = lax.dot_general(key_block(j, hh), qs[hh], NT_DIMS, preferred_element_type=F32)

    def update(state, s, vt_h):
        m, acc = state
        m_new = jnp.maximum(m, jnp.max(s, axis=0, keepdims=True))
        p = jnp.exp2(s - m_new).astype(BF16)
        return m_new, jnp.exp2(m - m_new) * acc + jnp.dot(vt_h, p, preferred_element_type=F32)

    def consume(slot, j, carry):
        vt = vt_ref[j]
        return tuple(update(carry[hh], s_scr[slot, hh], vt[head_rows[hh], :]) for hh in range(n_heads))

    def pair(i, carry):
        scores_into(1, 2 * i + 1)
        carry = consume(0, 2 * i, carry)
        scores_into(0, 2 * i + 2)
        return consume(1, 2 * i + 1, carry)

    init = tuple((jnp.full((1, tq), -jnp.inf, F32), jnp.zeros((V_AUG, tq), F32)) for _ in range(n_heads))
    scores_into(0, 0)
    carry = lax.fori_loop(0, qi, pair, init)

    vt_a = vt_ref[2 * qi]
    vt_b = vt_ref[2 * qi + 1]
    outs = []
    for hh in range(n_heads):
        m, acc = carry[hh]
        s_a = s_scr[0, hh]
        s_b = lax.dot_general(key_block(2 * qi + 1, hh), qs[hh][tk:, :], NT_DIMS, preferred_element_type=F32)
        va, vb = vt_a[head_rows[hh], :], vt_b[head_rows[hh], :]
        _, acc_l = update((m[:, :tk], acc[:, :tk]), jnp.where(visible, s_a[:, :tk], neg), va)
        right = update((m[:, tk:], acc[:, tk:]), s_a[:, tk:], va)
        _, acc_r = update(right, jnp.where(visible, s_b, neg), vb)
        acc = jnp.concatenate([acc_l, acc_r], axis=1)
        outs.append(acc[:HEAD_DIM] * (1.0 / acc[HEAD_DIM:HEAD_DIM + 1]))
    o_ref[...] = jnp.concatenate(outs, axis=0).T.astype(BF16)


def _mla_attn(q, k, vt, batch, seq, tq):
    t = q.shape[0]
    group = ATTN_HEADS * HEAD_PAD
    n_pairs = q.shape[1] // group
    nq = seq // tq
    _, v_rows, tk = vt.shape
    nk = seq // tk
    return pl.pallas_call(
        _mla_attn_kernel,
        grid=(batch, n_pairs, nq),
        in_specs=[
            pl.BlockSpec((tq, group), lambda b, hp, i: (b * nq + i, hp)),
            pl.BlockSpec((seq, group), lambda b, hp, i: (b, hp)),
            pl.BlockSpec((nk, v_rows // n_pairs, tk), lambda b, hp, i: (b, hp, 0)),
        ],
        out_specs=pl.BlockSpec((tq, ATTN_HEADS * HEAD_DIM), lambda b, hp, i: (b * nq + i, hp)),
        out_shape=jax.ShapeDtypeStruct((t, n_pairs * ATTN_HEADS * HEAD_DIM), BF16),
        scratch_shapes=[pltpu.VMEM((2, ATTN_HEADS, tk, tq), F32)],
        compiler_params=pltpu.CompilerParams(dimension_semantics=("parallel", "parallel", "parallel"),
                                             vmem_limit_bytes=VMEM_LIMIT),
        name="mla_attn",
    )(q, k, vt)


N_MIX_OPERANDS = 6


def _mixer_residual(x_ref, omla_ref, omem_ref, oconv_ref, g_ref, w_ref):
    a = _rms(omla_ref[...].astype(F32), g_ref[...]).astype(BF16)
    cat = jnp.concatenate([a, omem_ref[...], oconv_ref[...]], axis=-1)
    return x_ref[...] + jnp.dot(cat, w_ref[...], preferred_element_type=F32)


def _mixer_specs(mix, tm):
    row = lambda i: (i, 0)
    const = lambda i: (0, 0)
    return ([pl.BlockSpec((tm, a.shape[1]), row) for a in mix[:4]]
            + [pl.BlockSpec(mix[4].shape, const), pl.BlockSpec(mix[5].shape, const, pipeline_mode=pl.Buffered(1))])


def _out_proj_kernel(*refs):
    refs[N_MIX_OPERANDS][...] = _mixer_residual(*refs[:N_MIX_OPERANDS])


def _out_proj(mix, tm):
    t, d = mix[0].shape
    return pl.pallas_call(
        _out_proj_kernel,
        grid=(t // tm,),
        in_specs=_mixer_specs(mix, tm),
        out_specs=pl.BlockSpec((tm, d), lambda i: (i, 0)),
        out_shape=jax.ShapeDtypeStruct((t, d), F32),
        compiler_params=pltpu.CompilerParams(dimension_semantics=("parallel",), vmem_limit_bytes=VMEM_LIMIT),
        name="out_proj",
    )(*mix)


def _swiglu_part(h, wgu_ref, wd_ref, chunks):
    ff = wd_ref.shape[0]
    out = None
    for a, b in chunks:
        if (a, b) == (0, ff):
            gu = jnp.dot(h, wgu_ref[...], preferred_element_type=F32)
            g, u = gu[:, :ff], gu[:, ff:]
        else:
            g = jnp.dot(h, wgu_ref[:, a:b], preferred_element_type=F32)
            u = jnp.dot(h, wgu_ref[:, ff + a:ff + b], preferred_element_type=F32)
        act = (g * (1.0 / (1.0 + jnp.exp(-g))) * u).astype(BF16)
        part = jnp.dot(act, wd_ref[a:b, :], preferred_element_type=F32)
        out = part if out is None else out + part
    return out


def _dense_ffn_kernel(*refs, chunks):
    mix_refs, (g_ref, wgu_ref, wd_ref, y_ref) = refs[:N_MIX_OPERANDS], refs[N_MIX_OPERANDS:]
    x = _mixer_residual(*mix_refs)
    h = _rms(x, g_ref[...]).astype(BF16)
    y_ref[...] = x + _swiglu_part(h, wgu_ref, wd_ref, chunks)


def _dense_ffn(mix, g_ffn, wgu, wd, tm, chunks):
    t, d = mix[0].shape
    row = lambda i: (i, 0)
    const = lambda i: (0, 0)
    resident = lambda a: pl.BlockSpec(a.shape, const, pipeline_mode=pl.Buffered(1))
    return pl.pallas_call(
        functools.partial(_dense_ffn_kernel, chunks=chunks),
        grid=(t // tm,),
        in_specs=_mixer_specs(mix, tm) + [pl.BlockSpec(g_ffn.shape, const), resident(wgu), resident(wd)],
        out_specs=pl.BlockSpec((tm, d), row),
        out_shape=jax.ShapeDtypeStruct((t, d), F32),
        compiler_params=pltpu.CompilerParams(dimension_semantics=("parallel",), vmem_limit_bytes=VMEM_LIMIT),
        name="dense_ffn",
    )(*mix, g_ffn, wgu, wd)


ROUTE_ROWS = 8
TOP_K = 2
DMA_UNROLL = 8
ROW_TILE = 8


def _moe_route_kernel(x_ref, g_ref, wr_ref, tri_ref, rw_ref, ri_ref, cnt_ref, base_scr):
    tm = x_ref.shape[0]
    lane = lax.broadcasted_iota(jnp.int32, (tm, LANES), 1)
    lane_f = lane.astype(F32)

    @pl.when(pl.program_id(0) == 0)
    def _():
        base_scr[...] = jnp.zeros_like(base_scr)

    hf = _rms(x_ref[...], g_ref[...])
    wr = wr_ref[...]
    h_hi = hf.astype(BF16)
    h_lo = (hf - h_hi.astype(F32)).astype(BF16)
    w_hi = wr.astype(BF16)
    w_lo = (wr - w_hi.astype(F32)).astype(BF16)
    hi_both = jnp.dot(h_hi, jnp.concatenate([w_hi, w_lo], axis=1), preferred_element_type=F32)
    logits = hi_both[:, :LANES] + hi_both[:, LANES:] + jnp.dot(h_lo, w_hi, preferred_element_type=F32)
    logits = jnp.where(lane < N_EXPERTS, logits, -jnp.inf)
    m1 = jnp.max(logits, axis=-1, keepdims=True)
    i1 = jnp.min(jnp.where(logits == m1, lane_f, float(LANES)), axis=-1, keepdims=True)
    rest = jnp.where(lane_f == i1, -jnp.inf, logits)
    m2 = jnp.max(rest, axis=-1, keepdims=True)
    i2 = jnp.min(jnp.where(rest == m2, lane_f, float(LANES)), axis=-1, keepdims=True)
    e2 = jnp.exp(m2 - m1)
    w1 = 1.0 / (1.0 + e2)
    rw_ref[...] = jnp.where(lane == 0, w1, jnp.where(lane == 1, e2 * w1, 0.0))

    chosen = jnp.where((lane_f == i1) | (lane_f == i2), 1.0, 0.0)
    before = jnp.dot(tri_ref[...], chosen.astype(BF16), preferred_element_type=F32) + base_scr[...]
    r1 = jnp.sum(jnp.where(lane_f == i1, before, 0.0), axis=-1, keepdims=True)
    r2 = jnp.sum(jnp.where(lane_f == i2, before, 0.0), axis=-1, keepdims=True)
    base_scr[...] += jnp.sum(chosen, axis=0, keepdims=True)
    cnt_ref[...] = base_scr[...].astype(jnp.int32)
    table = jnp.where(lane == 0, i1, jnp.where(lane == 1, i2, jnp.where(lane == 2, r1, jnp.where(lane == 3, r2, 0.0))))
    ri_ref[...] = table.T[:ROUTE_ROWS, :].astype(jnp.int32)


def _moe_route(x2d, g_ffn, w_router, tm):
    t, d = x2d.shape
    tri = (jnp.arange(tm)[:, None] > jnp.arange(tm)[None, :]).astype(BF16)
    const = lambda i: (0, 0)
    return pl.pallas_call(
        _moe_route_kernel,
        grid=(t // tm,),
        in_specs=[pl.BlockSpec((tm, d), lambda i: (i, 0)), pl.BlockSpec(g_ffn.shape, const),
                  pl.BlockSpec(w_router.shape, const), pl.BlockSpec((tm, tm), const)],
        out_specs=[pl.BlockSpec((tm, LANES), lambda i: (i, 0)),
                   pl.BlockSpec((None, ROUTE_ROWS, tm), lambda i: (i, 0, 0)),
                   pl.BlockSpec((1, LANES), const)],
        out_shape=[jax.ShapeDtypeStruct((t, LANES), F32),
                   jax.ShapeDtypeStruct((t // tm, ROUTE_ROWS, tm), jnp.int32),
                   jax.ShapeDtypeStruct((1, LANES), jnp.int32)],
        scratch_shapes=[pltpu.VMEM((1, LANES), F32)],
        compiler_params=pltpu.CompilerParams(dimension_semantics=("arbitrary",), vmem_limit_bytes=VMEM_LIMIT),
        name="moe_route",
    )(x2d, g_ffn, w_router, tri)


def _token_copy(src_ref, src_tok, dst_ref, dst_tok, sem):
    src = src_ref.at[pl.ds(pl.multiple_of(src_tok * ROW_TILE, ROW_TILE), ROW_TILE)]
    dst = dst_ref.at[pl.ds(pl.multiple_of(dst_tok * ROW_TILE, ROW_TILE), ROW_TILE)]
    return pltpu.make_async_copy(src, dst, sem)


def _to_token_tiles(ref, value):
    rows = value.shape[0]
    for c in range(ROW_TILE):
        ref[pl.ds(c, rows, stride=ROW_TILE), :] = value[:, c * LANES:(c + 1) * LANES]


def _from_token_tiles(ref, rows):
    return jnp.concatenate([ref[pl.ds(c, rows, stride=ROW_TILE), :] for c in range(ROW_TILE)], axis=-1)


def _moe_dispatch_kernel(zlo_ref, zhi_ref, x_ref, g_ref, pos_ref, xs_ref, h_scr, zero_scr, sem):
    tm = x_ref.shape[0]
    _to_token_tiles(h_scr, _rms(x_ref[...], g_ref[...]))

    def start(r, c):
        for k in range(TOP_K):
            _token_copy(h_scr, r, xs_ref, pos_ref[0, k * tm + r], sem).start(priority=k)
        return c

    def wait(r, c):
        for k in range(TOP_K):
            _token_copy(h_scr, 0, xs_ref, 0, sem).wait()
        return c

    lax.fori_loop(0, tm, start, 0, unroll=DMA_UNROLL)
    lax.fori_loop(0, tm, wait, 0, unroll=DMA_UNROLL)

    @pl.when(pl.program_id(0) == pl.num_programs(0) - 1)
    def _():
        zero_scr[...] = jnp.zeros_like(zero_scr)
        for e in range(N_EXPERTS):
            def zstart(r, c):
                _token_copy(zero_scr, 0, xs_ref, r, sem).start()
                return c

            def zwait(r, c):
                _token_copy(zero_scr, 0, xs_ref, 0, sem).wait()
                return c

            lax.fori_loop(zlo_ref[e], zhi_ref[e], zstart, 0)
            lax.fori_loop(zlo_ref[e], zhi_ref[e], zwait, 0)


def _moe_dispatch(x2d, g_ffn, pos, zero_lo, zero_hi, n_rows, tm):
    t, d = x2d.shape
    assert d == ROW_TILE * LANES
    return pl.pallas_call(
        _moe_dispatch_kernel,
        grid_spec=pltpu.PrefetchScalarGridSpec(
            num_scalar_prefetch=2,
            grid=(t // tm,),
            in_specs=[pl.BlockSpec((tm, d), lambda i, *_: (i, 0)),
                      pl.BlockSpec(g_ffn.shape, lambda i, *_: (0, 0)),
                      pl.BlockSpec((None, 1, TOP_K * tm), lambda i, *_: (i, 0, 0), memory_space=pltpu.SMEM)],
            out_specs=pl.BlockSpec(memory_space=pl.ANY),
            scratch_shapes=[pltpu.VMEM((tm * ROW_TILE, LANES), F32), pltpu.VMEM((ROW_TILE, LANES), F32),
                            pltpu.SemaphoreType.DMA(())],
        ),
        out_shape=jax.ShapeDtypeStruct((n_rows * ROW_TILE, LANES), F32),
        compiler_params=pltpu.CompilerParams(dimension_semantics=("arbitrary",), vmem_limit_bytes=VMEM_LIMIT),
        name="moe_dispatch",
    )(zero_lo, zero_hi, x2d, g_ffn, pos)


def _moe_gmm_kernel(te_ref, nu_ref, xs_ref, wgu_ref, wd_ref, ys_ref, *, chunks):
    del te_ref
    tg = xs_ref.shape[0] // ROW_TILE
    used = pl.program_id(0) < nu_ref[0]

    @pl.when(used)
    def _():
        x = _from_token_tiles(xs_ref, tg).astype(BF16)
        _to_token_tiles(ys_ref, _swiglu_part(x, wgu_ref, wd_ref, chunks))

    @pl.when(jnp.logical_not(used))
    def _():
        ys_ref[...] = jnp.zeros_like(ys_ref)


def _moe_gmm(xs, tile_expert, n_used, wgu, wd, tg, chunks):
    _, ff, d = wd.shape
    n_tiles = xs.shape[0] // (tg * ROW_TILE)
    row = lambda i, te, nu: (i, 0)
    by_expert = lambda i, te, nu: (te[i], 0, 0)
    return pl.pallas_call(
        functools.partial(_moe_gmm_kernel, chunks=chunks),
        grid_spec=pltpu.PrefetchScalarGridSpec(
            num_scalar_prefetch=2,
            grid=(n_tiles,),
            in_specs=[pl.BlockSpec((tg * ROW_TILE, LANES), row),
                      pl.BlockSpec((None, d, 2 * ff), by_expert),
                      pl.BlockSpec((None, ff, d), by_expert)],
            out_specs=pl.BlockSpec((tg * ROW_TILE, LANES), row),
        ),
        out_shape=jax.ShapeDtypeStruct(xs.shape, F32),
        compiler_params=pltpu.CompilerParams(dimension_semantics=("arbitrary",), vmem_limit_bytes=VMEM_LIMIT),
        name="moe_gmm",
    )(tile_expert, n_used, xs, wgu, wd)


def _moe_combine_kernel(x_ref, rw_ref, pos_ref, ys_ref, y_ref, a_scr, b_scr, sem):
    tm = x_ref.shape[0]
    bufs = (a_scr, b_scr)

    def start(r, c):
        for k in range(TOP_K):
            _token_copy(ys_ref, pos_ref[0, k * tm + r], bufs[k], r, sem).start(priority=k)
        return c

    def wait(r, c):
        for k in range(TOP_K):
            _token_copy(ys_ref, 0, bufs[k], 0, sem).wait()
        return c

    lax.fori_loop(0, tm, start, 0, unroll=DMA_UNROLL)
    lax.fori_loop(0, tm, wait, 0, unroll=DMA_UNROLL)
    rw = rw_ref[...]
    y_ref[...] = x_ref[...] + rw[:, 0:1] * _from_token_tiles(a_scr, tm) + rw[:, 1:2] * _from_token_tiles(b_scr, tm)


def _moe_combine(x2d, route_w, pos, ys, tm):
    t, d = x2d.shape
    return pl.pallas_call(
        _moe_combine_kernel,
        grid=(t // tm,),
        in_specs=[pl.BlockSpec((tm, d), lambda i: (i, 0)),
                  pl.BlockSpec((tm, LANES), lambda i: (i, 0)),
                  pl.BlockSpec((None, 1, TOP_K * tm), lambda i: (i, 0, 0), memory_space=pltpu.SMEM),
                  pl.BlockSpec(memory_space=pl.ANY)],
        out_specs=pl.BlockSpec((tm, d), lambda i: (i, 0)),
        out_shape=jax.ShapeDtypeStruct((t, d), F32),
        scratch_shapes=[pltpu.VMEM((tm * ROW_TILE, LANES), F32), pltpu.VMEM((tm * ROW_TILE, LANES), F32),
                        pltpu.SemaphoreType.DMA(())],
        compiler_params=pltpu.CompilerParams(dimension_semantics=("arbitrary",), vmem_limit_bytes=VMEM_LIMIT),
        name="moe_combine",
    )(x2d, route_w, pos, ys)


def _moe_ffn(x2d, g_ffn, w_router, wgu, wd, tm, tg, chunks):
    t, _ = x2d.shape
    n_e = wd.shape[0]
    route_w, route_i, counts = _moe_route(x2d, g_ffn, w_router, tm)
    cnt = counts[0, :n_e]
    tiles = (cnt + tg - 1) // tg
    tile_end = jnp.cumsum(tiles)
    off = ((tile_end - tiles) * tg).astype(jnp.int32)
    n_tiles = (TOP_K * t) // tg + n_e
    tile_ids = jnp.arange(n_tiles, dtype=jnp.int32)
    tile_expert = jnp.sum((tile_ids[:, None] >= tile_end[None, :]).astype(jnp.int32), axis=1)
    tile_expert = jnp.minimum(tile_expert, n_e - 1).astype(jnp.int32)
    zero_hi = (tile_end * tg).astype(jnp.int32).at[n_e - 1].set(n_tiles * tg)
    pos = jnp.take(off, route_i[:, :TOP_K, :]) + route_i[:, TOP_K:2 * TOP_K, :]
    pos = pos.reshape(t // tm, 1, TOP_K * tm)
    xs = _moe_dispatch(x2d, g_ffn, pos, off + cnt, zero_hi, n_tiles * tg, tm)
    ys = _moe_gmm(xs, tile_expert, tile_end[-1:].astype(jnp.int32), wgu, wd, tg, chunks)
    return _moe_combine(x2d, route_w, pos, ys, tm)


def _segment_matrix(bounds):
    seg_id = jnp.zeros((bounds[-1],), jnp.int32)
    for b in bounds[1:-1]:
        seg_id = seg_id + (jnp.arange(bounds[-1]) >= b).astype(jnp.int32)
    return (seg_id[:, None] == seg_id[None, :]).astype(BF16)


def _constants(seq):
    head_bounds = [0, HEAD_DIM, HEAD_DIM + ROPE_DIM, HEAD_PAD]
    pair_bounds = head_bounds + [HEAD_PAD + b for b in head_bounds[1:]]
    inv_head = jnp.concatenate([jnp.full((HEAD_DIM,), 1.0 / HEAD_DIM, F32),
                                jnp.full((HEAD_PAD - HEAD_DIM,), 1.0 / ROPE_DIM, F32)])
    half = ROPE_DIM // 2
    inv = 1.0 / (ROPE_THETA ** (jnp.arange(0, ROPE_DIM, 2, dtype=F32) / ROPE_DIM))
    ang = jnp.arange(seq, dtype=F32)[:, None] * inv[None, :]
    ones = jnp.ones((seq, HEAD_DIM), F32)
    zeros = jnp.zeros((seq, HEAD_DIM), F32)
    pad1 = jnp.ones((seq, HEAD_PAD - HEAD_DIM - ROPE_DIM), F32)
    cos = jnp.concatenate([ones, jnp.cos(ang), jnp.cos(ang), pad1], axis=-1)
    sin = jnp.concatenate([zeros, -jnp.sin(ang), jnp.sin(ang), 0.0 * pad1], axis=-1)
    assert cos.shape == (seq, HEAD_PAD) and half * 2 == ROPE_DIM
    return {
        "seg_mla": _segment_matrix(pair_bounds),
        "inv_mla": jnp.concatenate([inv_head, inv_head])[None, :],
        "seg64": _segment_matrix(list(range(0, PAIR + 1, HEAD_DIM))),
        "v_ones": jnp.broadcast_to(
            jnp.tile(jnp.concatenate([jnp.zeros((HEAD_DIM,), F32), jnp.ones((SUM_ROWS,), F32)]), N_MLA_HEADS)[:, None],
            (N_MLA_HEADS * V_AUG, LANES)),
        "cos": cos,
        "sin": sin,
    }


def _layer_weights(l, p):
    d = p["w_in"].shape[1]
    q_rank = p["g_q_lat"].shape[1]
    kv_rank = p["g_kv_lat"].shape[1]
    qk_dim = HEAD_DIM + ROPE_DIM
    pad = HEAD_PAD - qk_dim
    w_in = p["w_in"][l]
    o = q_rank + kv_rank
    w_kpe = w_in[:, o:o + ROPE_DIM]
    w_in_p = jnp.concatenate([
        w_in[:, :o],
        jnp.zeros((d, HEAD_DIM), F32), w_kpe, jnp.zeros((d, pad), F32),
        w_in[:, o + ROPE_DIM:],
    ], axis=1).astype(BF16)

    w_q = p["w_q_up"][l].reshape(q_rank, N_MLA_HEADS, qk_dim)
    w_q = jnp.pad(w_q, ((0, 0), (0, 0), (0, pad))).reshape(q_rank, N_MLA_HEADS * HEAD_PAD).astype(BF16)
    w_kv = p["w_kv_up"][l].reshape(kv_rank, N_MLA_HEADS, 2 * HEAD_DIM)
    w_k = jnp.pad(w_kv[:, :, :HEAD_DIM], ((0, 0), (0, 0), (0, HEAD_PAD - HEAD_DIM)))
    w_k = w_k.reshape(kv_rank, N_MLA_HEADS * HEAD_PAD).astype(BF16)
    w_v = jnp.pad(w_kv[:, :, HEAD_DIM:], ((0, 0), (0, 0), (0, SUM_ROWS)))
    w_v = w_v.reshape(kv_rank, N_MLA_HEADS * V_AUG).astype(BF16)

    gq = jnp.pad(p["g_q_mla"][l], (0, pad))
    gk = jnp.pad(p["g_k_mla"][l][:HEAD_DIM], (0, HEAD_PAD - HEAD_DIM))
    gkpe = jnp.pad(p["g_k_mla"][l][HEAD_DIM:], (HEAD_DIM, pad))
    g_out = p["g_out"][l]
    mla_w = N_MLA_HEADS * HEAD_DIM
    mem_w = N_MEM_HEADS * HEAD_DIM
    return {
        "g_mix": p["g_mix"][l][None, :],
        "w_in": w_in_p,
        "g_q_lat": p["g_q_lat"][l][None, :],
        "w_q": w_q,
        "g_kv_lat": p["g_kv_lat"][l][None, :],
        "w_k": w_k,
        "w_v_t": w_v.T,
        "gq_row": jnp.tile(gq * Q_SCALE, N_MLA_HEADS)[None, :],
        "gk_row": jnp.tile(gk, N_MLA_HEADS)[None, :],
        "gkpe_row": gkpe[None, :],
        "g_mem": p["g_mem"][l][None, :],
        "w_mem_kv": p["w_mem_kv"][l].astype(BF16),
        "gqm_row": jnp.tile(p["g_q_mem"][l], N_MEM_HEADS)[None, :],
        "gkm_row": jnp.tile(p["g_k_mem"][l], N_MEM_HEADS)[None, :],
        "conv_w": p["conv_w"][l],
        "g_out_mla": g_out[None, :mla_w],
        "g_out_mem": g_out[None, mla_w:mla_w + mem_w],
        "g_out_conv": g_out[None, mla_w + mem_w:],
        "w_out": p["w_out"][l].astype(BF16),
        "g_ffn": p["g_ffn"][l][None, :],
    }


def _tile_config(seq):
    return dict(tm=min(1024, seq), proj_sub=min(512, seq), tq=min(512, seq), tk=min(256, seq), tm_ffn=512,
                ffn_chunks=((0, 1536), (1536, 2816)), moe_chunks=((0, 1408),))


def _forward(p, *, tm, proj_sub, tq, tk, tm_ffn, ffn_chunks, moe_chunks):
    x = p["x"]
    batch, seq, d = x.shape
    depth = p["g_mix"].shape[0]
    assert seq % tm == 0 and seq % tq == 0 and (batch * seq) % tm_ffn == 0
    consts = _constants(seq)
    x2d = x.reshape(batch * seq, d)
    for l in range(depth):
        lw = _layer_weights(l, p)
        km, vm = _mem_kv(p["mem"], lw["g_mem"], lw["w_mem_kv"], consts["seg64"], lw["gkm_row"])
        q, k, vt, o_mem, o_conv = _proj(x2d, seq, tm, proj_sub, tk, lw, km, vm, consts["cos"], consts["sin"], consts)
        o_mla = _mla_attn(q, k, vt, batch, seq, tq)
        mix = (x2d, o_mla, o_mem, o_conv, lw["g_out_mla"], lw["w_out"])
        assert len(mix) == N_MIX_OPERANDS
        if l % 2 == 0:
            x2d = _dense_ffn(mix, lw["g_ffn"], p["w_dense_gu"][l // 2].astype(BF16),
                             p["w_dense_down"][l // 2].astype(BF16), tm_ffn, ffn_chunks)
        else:
            w_router = jnp.pad(p["w_router"][l // 2], ((0, 0), (0, LANES - N_EXPERTS)))
            x2d = _out_proj(mix, tm)
            x2d = _moe_ffn(x2d, lw["g_ffn"], w_router, p["w_expert_gu"][l // 2].astype(BF16),
                           p["w_expert_down"][l // 2].astype(BF16), tm_ffn, tm_ffn, moe_chunks)
    return x2d.reshape(batch, seq, d)


def kernel(x, mem, g_mix, w_in, g_q_lat, w_q_up, g_kv_lat, w_kv_up, g_q_mla, g_k_mla, g_mem, w_mem_kv, g_q_mem,
           g_k_mem, conv_w, g_out, w_out, g_ffn, w_dense_gu, w_dense_down, w_router, w_expert_gu, w_expert_down):
    p = dict(x=x, mem=mem, g_mix=g_mix, w_in=w_in, g_q_lat=g_q_lat, w_q_up=w_q_up, g_kv_lat=g_kv_lat,
             w_kv_up=w_kv_up, g_q_mla=g_q_mla, g_k_mla=g_k_mla, g_mem=g_mem, w_mem_kv=w_mem_kv, g_q_mem=g_q_mem,
             g_k_mem=g_k_mem, conv_w=conv_w, g_out=g_out, w_out=w_out, g_ffn=g_ffn, w_dense_gu=w_dense_gu,
             w_dense_down=w_dense_down, w_router=w_router, w_expert_gu=w_expert_gu, w_expert_down=w_expert_down)
    return _forward(p, **_tile_config(x.shape[1]))
```
